```python
import math
import jax, jax.numpy as jnp
from jax import lax
import numpy as np


D_MODEL = 1024
BATCH = 8
SEQ = 4096
DEPTH = 2

CHUNK = 64
N_MIXERS = 2
N_POOL_LAYERS = (DEPTH + N_MIXERS - 1) // N_MIXERS
N_SB_LAYERS = DEPTH // N_MIXERS

POOL_WINDOWS = (2, 4, 8, 16)
N_POOL_GROUPS = len(POOL_WINDOWS)
POOL_GROUP_W = D_MODEL // N_POOL_GROUPS

N_HEADS = 16
HEAD_DIM = D_MODEL // N_HEADS
Q_BLOCK = 128

D_FF = ((8 * D_MODEL // 3 + 255) // 256) * 256

DEEPNORM_ALPHA = (2.0 * DEPTH) ** 0.25
DEEPNORM_BETA = (8.0 * DEPTH) ** -0.25
LN_EPS = 1e-5

kernel_name = "hybrid_pool_stickbreak_deepnorm_trunk"


def _layer_norm(x, g, b):
    xf = x.astype(jnp.float32)
    mu = jnp.mean(xf, axis=-1, keepdims=True)
    var = jnp.mean(jnp.square(xf - mu), axis=-1, keepdims=True)
    y = (xf - mu) * lax.rsqrt(var + LN_EPS) * g.astype(jnp.float32) + b.astype(jnp.float32)
    return y.astype(x.dtype)


def _pool_mixer(x, w_grp, scale):
    B, S, D = x.shape
    xg = x.reshape(B, S, N_POOL_GROUPS, POOL_GROUP_W)
    xf = xg.astype(jnp.float32)
    c = jnp.cumsum(xf, axis=1)
    c = jnp.concatenate([jnp.zeros((B, 1, N_POOL_GROUPS, POOL_GROUP_W), jnp.float32), c], axis=1)
    t = jnp.arange(S)
    pooled = []
    for g, w in enumerate(POOL_WINDOWS):
        cg = c[:, :, g]
        hi = cg[:, 1:]
        lo = jnp.pad(cg, ((0, 0), (w - 1, 0), (0, 0)))[:, :S]
        cnt = jnp.minimum(t + 1, w).astype(jnp.float32)[None, :, None]
        pooled.append((hi - lo) / cnt)
    pooled = jnp.stack(pooled, axis=2)
    mix = (pooled - xf).astype(x.dtype)
    y = jnp.einsum('bsgc,gcd->bsgd', mix, w_grp).reshape(B, S, D)
    return y * scale


def _stick_breaking_attention(x, w_qkv, w_o):
    B, S, D = x.shape
    qkv = jnp.einsum('bsd,de->bse', x, w_qkv).reshape(B, S, 3, N_HEADS, HEAD_DIM)
    q = jnp.transpose(qkv[:, :, 0], (0, 2, 1, 3))
    k = jnp.transpose(qkv[:, :, 1], (0, 2, 1, 3))
    v = jnp.transpose(qkv[:, :, 2], (0, 2, 1, 3))
    inv_sqrt_d = 1.0 / math.sqrt(HEAD_DIM)
    outs = []
    for blk in range(S // Q_BLOCK):
        q0 = blk * Q_BLOCK
        q1 = q0 + Q_BLOCK
        qb = q[:, :, q0:q1]
        kb = k[:, :, :q1]
        vb = v[:, :, :q1]
        z = jnp.einsum('bhqd,bhkd->bhqk', qb, kb).astype(jnp.float32) * inv_sqrt_d
        qpos = (q0 + jnp.arange(Q_BLOCK))[:, None]
        kpos = jnp.arange(q1)[None, :]
        mask = kpos < qpos
        log_beta = jax.nn.log_sigmoid(z)
        log_1mb = jnp.where(mask, jax.nn.log_sigmoid(-z), 0.0)
        suffix = lax.cumsum(log_1mb, axis=3, reverse=True) - log_1mb
        a = jnp.where(mask, jnp.exp(log_beta + suffix), 0.0)
        outs.append(jnp.einsum('bhqk,bhkd->bhqd', a.astype(vb.dtype), vb))
    o = jnp.concatenate(outs, axis=2)
    o = jnp.transpose(o, (0, 2, 1, 3)).reshape(B, S, D)
    return jnp.einsum('bsd,de->bse', o, w_o)


def _swiglu(x, w_gate, w_up, w_down):
    h = jax.nn.silu(jnp.einsum('bsd,df->bsf', x, w_gate)) * jnp.einsum('bsd,df->bsf', x, w_up)
    return jnp.einsum('bsf,fd->bsd', h, w_down)


def setup_inputs(seed: int = 0) -> dict:
    key = jax.random.key(seed)
    ks = jax.random.split(key, 16)
    f32 = jnp.float32
    D, F, C = D_MODEL, D_FF, POOL_GROUP_W
    x = jax.random.normal(ks[0], (BATCH, SEQ, D), f32)
    ln_mix_g = 1.0 + 0.02 * jax.random.normal(ks[1], (DEPTH, D), f32)
    ln_mix_b = 0.02 * jax.random.normal(ks[2], (DEPTH, D), f32)
    ln_ffn_g = 1.0 + 0.02 * jax.random.normal(ks[3], (DEPTH, D), f32)
    ln_ffn_b = 0.02 * jax.random.normal(ks[4], (DEPTH, D), f32)
    pool_w = jax.random.normal(ks[5], (N_POOL_LAYERS, N_POOL_GROUPS, C, C), f32) * (C ** -0.5) * DEEPNORM_BETA
    pool_scale = 1.0 + 0.02 * jax.random.normal(ks[6], (N_POOL_LAYERS, D), f32)
    w_qk = jax.random.normal(ks[7], (N_SB_LAYERS, D, 2 * D), f32) * (D ** -0.5)
    w_v = jax.random.normal(ks[8], (N_SB_LAYERS, D, D), f32) * (D ** -0.5) * DEEPNORM_BETA
    w_qkv = jnp.concatenate([w_qk, w_v], axis=-1)
    w_o = jax.random.normal(ks[9], (N_SB_LAYERS, D, D), f32) * (D ** -0.5) * DEEPNORM_BETA
    w_gate = jax.random.normal(ks[10], (DEPTH, D, F), f32) * (D ** -0.5)
    w_up = jax.random.normal(ks[11], (DEPTH, D, F), f32) * (D ** -0.5) * DEEPNORM_BETA
    w_down = jax.random.normal(ks[12], (DEPTH, F, D), f32) * (F ** -0.5) * DEEPNORM_BETA
    return {"x": x, "ln_mix_g": ln_mix_g, "ln_mix_b": ln_mix_b, "ln_ffn_g": ln_ffn_g, "ln_ffn_b": ln_ffn_b,
            "pool_w": pool_w, "pool_scale": pool_scale, "w_qkv": w_qkv, "w_o": w_o,
            "w_gate": w_gate, "w_up": w_up, "w_down": w_down}


def reference(x, ln_mix_g, ln_mix_b, ln_ffn_g, ln_ffn_b, pool_w, pool_scale, w_qkv, w_o, w_gate, w_up, w_down):
    for i in range(DEPTH):
        j = i // N_MIXERS
        if i % N_MIXERS == 0:
            m = _pool_mixer(x, pool_w[j], pool_scale[j])
        else:
            m = _stick_breaking_attention(x, w_qkv[j], w_o[j])
        x = _layer_norm(DEEPNORM_ALPHA * x + m, ln_mix_g[i], ln_mix_b[i])
        f = _swiglu(x, w_gate[i], w_up[i], w_down[i])
        x = _layer_norm(DEEPNORM_ALPHA * x + f, ln_ffn_g[i], ln_ffn_b[i])
    return x
```

```python
import functools

import jax
import jax.numpy as jnp
from jax import lax
from jax.experimental import pallas as pl
from jax.experimental.pallas import tpu as pltpu

D_MODEL = 1024
DEPTH = 2
POOL_WINDOWS = (2, 4, 8, 16)
N_POOL_GROUPS = len(POOL_WINDOWS)
POOL_GROUP_W = D_MODEL // N_POOL_GROUPS
N_HEADS = 16
HEAD_DIM = D_MODEL // N_HEADS
D_FF = 2816
DEEPNORM_ALPHA = (2.0 * DEPTH) ** 0.25
LN_EPS = 1e-5

F32 = jnp.float32
BF16 = jnp.bfloat16

LANES = 128
VMEM_LIMIT_BYTES = 56 * 1024 * 1024

TOKEN_TILE = 512
POOL_HALO = 16
FF_CHUNK = 256
N_FF_CHUNKS = D_FF // FF_CHUNK
QKV_CHUNK = 512

HEADS_PER_SLAB = LANES // HEAD_DIM
N_SLABS = N_HEADS // HEADS_PER_SLAB
ATT_BLOCK = 128
EXP_UNDERFLOW = 104.0


def _layer_norm(v, g, b):
    mu = jnp.mean(v, axis=-1, keepdims=True)
    c = v - mu
    var = jnp.mean(c * c, axis=-1, keepdims=True)
    return c * lax.rsqrt(var + LN_EPS) * g + b


def _resident(shape):
    nd = len(shape)
    return pl.BlockSpec(shape, lambda *_: (0,) * nd, pipeline_mode=pl.Buffered(1))


def _params(*sem):
    return pltpu.CompilerParams(dimension_semantics=sem, vmem_limit_bytes=VMEM_LIMIT_BYTES)


def _pool_ln_kernel(x_ref, halo_ref, w_ref, scale_ref, g_ref, b_ref, o_ref):
    i = pl.program_id(1)
    x = x_ref[0]
    halo = jnp.where(i > 0, halo_ref[0], 0.0)
    t = i * TOKEN_TILE + lax.broadcasted_iota(jnp.int32, (TOKEN_TILE, 1), 0)
    ys = []
    for gi, w in enumerate(POOL_WINDOWS):
        cols = slice(gi * POOL_GROUP_W, (gi + 1) * POOL_GROUP_W)
        xg = x[:, cols]
        s = jnp.concatenate([halo[:, cols], xg], axis=0)
        k = 1
        while k < w:
            s = s + pltpu.roll(s, k, axis=0)
            k *= 2
        cnt = jnp.minimum(t + 1, w).astype(F32)
        mix = s[POOL_HALO:, :] / cnt - xg
        ys.append(jnp.dot(mix.astype(BF16), w_ref[gi], preferred_element_type=F32))
    y = jnp.concatenate(ys, axis=1) * scale_ref[...]
    o_ref[0] = _layer_norm(DEEPNORM_ALPHA * x + y, g_ref[...], b_ref[...])


def _pool_ln(x, w_grp, scale, g, b):
    B, S, D = x.shape
    halo_blocks_per_tile = TOKEN_TILE // POOL_HALO
    return pl.pallas_call(
        _pool_ln_kernel,
        grid=(B, S // TOKEN_TILE),
        in_specs=[
            pl.BlockSpec((1, TOKEN_TILE, D), lambda bi, i: (bi, i, 0)),
            pl.BlockSpec((1, POOL_HALO, D),
                         lambda bi, i: (bi, jnp.maximum(i * halo_blocks_per_tile - 1, 0), 0)),
            _resident(w_grp.shape), _resident(scale.shape), _resident(g.shape), _resident(b.shape),
        ],
        out_specs=pl.BlockSpec((1, TOKEN_TILE, D), lambda bi, i: (bi, i, 0)),
        out_shape=jax.ShapeDtypeStruct(x.shape, F32),
        compiler_params=_params("parallel", "parallel"),
        name="pool_ln",
    )(x, x, w_grp, scale, g, b)


def _ffn_ln_kernel(x_ref, wgu_ref, wd_ref, g_ref, b_ref, o_ref):
    x = x_ref[...]
    xb = x.astype(BF16)
    acc = None
    for c in range(N_FF_CHUNKS):
        gu = jnp.dot(xb, wgu_ref[c], preferred_element_type=F32)
        gate, up = gu[:, :FF_CHUNK], gu[:, FF_CHUNK:]
        h = (gate * jax.nn.sigmoid(gate) * up).astype(BF16)
        p = jnp.dot(h, wd_ref[c], preferred_element_type=F32)
        acc = p if acc is None else acc + p
    o_ref[...] = _layer_norm(DEEPNORM_ALPHA * x + acc, g_ref[...], b_ref[...])


def _ffn_ln(x2d, wgu, wd, g, b):
    T, D = x2d.shape
    row = pl.BlockSpec((TOKEN_TILE, D), lambda i: (i, 0))
    return pl.pallas_call(
        _ffn_ln_kernel,
        grid=(T // TOKEN_TILE,),
        in_specs=[row, _resident(wgu.shape), _resident(wd.shape), _resident(g.shape), _resident(b.shape)],
        out_specs=row,
        out_shape=jax.ShapeDtypeStruct(x2d.shape, F32),
        compiler_params=_params("parallel"),
        name="ffn_ln",
    )(x2d, wgu, wd, g, b)


def _qkv_kernel(x_ref, w_ref, o_ref):
    xb = x_ref[...].astype(BF16)
    for c in range(3 * D_MODEL // QKV_CHUNK):
        cols = slice(c * QKV_CHUNK, (c + 1) * QKV_CHUNK)
        y = jnp.dot(xb, w_ref[:, cols], preferred_element_type=F32)
        if (c + 1) * QKV_CHUNK <= D_MODEL:
            y = y * (HEAD_DIM ** -0.5)
        o_ref[:, cols] = y.astype(BF16)


def _qkv(x2d, w_qkv):
    T, D = x2d.shape
    return pl.pallas_call(
        _qkv_kernel,
        grid=(T // TOKEN_TILE,),
        in_specs=[pl.BlockSpec((TOKEN_TILE, D), lambda i: (i, 0)), _resident(w_qkv.shape)],
        out_specs=pl.BlockSpec((TOKEN_TILE, 3 * D), lambda i: (i, 0)),
        out_shape=jax.ShapeDtypeStruct((T, 3 * D), BF16),
        compiler_params=_params("parallel"),
        name="qkv_proj",
    )(x2d, w_qkv)


def _attn_kernel(q_ref, k_ref, v_ref, o_ref, acc_ref, carry_ref):
    n_blocks = q_ref.shape[1] // ATT_BLOCK
    lane = lax.broadcasted_iota(jnp.int32, (ATT_BLOCK, LANES), 1)
    first_head = lane < HEAD_DIM
    r2 = lax.broadcasted_iota(jnp.int32, (ATT_BLOCK, 2 * ATT_BLOCK), 0)
    c2 = lax.broadcasted_iota(jnp.int32, (ATT_BLOCK, 2 * ATT_BLOCK), 1)
    m2 = jnp.where((c2 >= ATT_BLOCK) | (r2 > c2), 1.0, 0.0).astype(BF16)
    qi = lax.broadcasted_iota(jnp.int32, (ATT_BLOCK, ATT_BLOCK), 0)
    ki = lax.broadcasted_iota(jnp.int32, (ATT_BLOCK, ATT_BLOCK), 1)
    causal = ki < qi

    def sweep(j, qh, h, diag):
        rows = pl.ds(pl.multiple_of(j * ATT_BLOCK, ATT_BLOCK), ATT_BLOCK)
        k = k_ref[0, rows, :]
        v = v_ref[0, rows, :]
        z = lax.dot_general(qh, k, (((1,), (1,)), ((), ())), preferred_element_type=F32)
        sp = jnp.maximum(z, 0.0) + jnp.log(1.0 + jnp.exp(-jnp.abs(z)))
        log_beta = z - sp
        if diag:
            sp = jnp.where(causal, sp, 0.0)
        hi = sp.astype(BF16)
        lo = (sp - hi.astype(F32)).astype(BF16)
        st = (jnp.dot(hi, m2, preferred_element_type=F32) + jnp.dot(lo, m2, preferred_element_type=F32))
        suffix, total = st[:, :ATT_BLOCK], st[:, ATT_BLOCK:]
        if diag:
            a = jnp.where(causal, jnp.exp(log_beta - suffix), 0.0)
            acc_ref[h] = jnp.dot(a.astype(BF16), v, preferred_element_type=F32)
            carry = total
        else:
            carry = carry_ref[h]
            a = jnp.exp(log_beta - suffix - carry)
            acc_ref[h] += jnp.dot(a.astype(BF16), v, preferred_element_type=F32)
            carry = carry + total
        carry_ref[h] = carry
        return jnp.min(carry)

    def q_block(i, _):
        rows = pl.ds(pl.multiple_of(i * ATT_BLOCK, ATT_BLOCK), ATT_BLOCK)
        q = q_ref[0, rows, :]
        zero = jnp.zeros_like(q)
        q_heads = (jnp.where(first_head, q, zero), jnp.where(first_head, zero, q))
        sweep(i, q_heads[0], 0, True)
        sweep(i, q_heads[1], 1, True)

        def more(state):
            j, least = state
            return jnp.logical_and(j >= 0, least <= EXP_UNDERFLOW)

        def step(state):
            j, _ = state
            least = jnp.minimum(sweep(j, q_heads[0], 0, False), sweep(j, q_heads[1], 1, False))
            return j - 1, least

        lax.while_loop(more, step, (i - 1, jnp.float32(0.0)))
        o_ref[0, rows, :] = jnp.where(first_head, acc_ref[0], acc_ref[1]).astype(BF16)
        return 0

    lax.fori_loop(0, n_blocks, q_block, 0)


def _attention(qkv):
    B, S, _ = qkv.shape
    def slab(which):
        return pl.BlockSpec((1, S, LANES), lambda bi, p: (bi, 0, which * N_SLABS + p))
    return pl.pallas_call(
        _attn_kernel,
        grid=(B, N_SLABS),
        in_specs=[slab(0), slab(1), slab(2)],
        out_specs=pl.BlockSpec((1, S, LANES), lambda bi, p: (bi, 0, p)),
        out_shape=jax.ShapeDtypeStruct((B, S, D_MODEL), BF16),
        scratch_shapes=[pltpu.VMEM((HEADS_PER_SLAB, ATT_BLOCK, LANES), F32),
                        pltpu.VMEM((HEADS_PER_SLAB, ATT_BLOCK, ATT_BLOCK), F32)],
        compiler_params=_params("parallel", "parallel"),
        name="stickbreak_attn",
    )(qkv, qkv, qkv)


def _proj_ln_kernel(o_ref, x_ref, w_ref, g_ref, b_ref, y_ref):
    m = jnp.dot(o_ref[...], w_ref[...], preferred_element_type=F32)
    y_ref[...] = _layer_norm(DEEPNORM_ALPHA * x_ref[...] + m, g_ref[...], b_ref[...])


def _proj_ln(o2d, x2d, w_o, g, b):
    T, D = x2d.shape
    row = pl.BlockSpec((TOKEN_TILE, D), lambda i: (i, 0))
    return pl.pallas_call(
        _proj_ln_kernel,
        grid=(T // TOKEN_TILE,),
        in_specs=[row, row, _resident(w_o.shape), _resident(g.shape), _resident(b.shape)],
        out_specs=row,
        out_shape=jax.ShapeDtypeStruct(x2d.shape, F32),
        compiler_params=_params("parallel"),
        name="proj_ln",
    )(o2d, x2d, w_o, g, b)


def _ffn_weights(w_gate, w_up, w_down):
    D = w_gate.shape[0]
    wg = w_gate.astype(BF16).reshape(D, N_FF_CHUNKS, FF_CHUNK)
    wu = w_up.astype(BF16).reshape(D, N_FF_CHUNKS, FF_CHUNK)
    wgu = jnp.transpose(jnp.concatenate([wg, wu], axis=2), (1, 0, 2))
    wd = w_down.astype(BF16).reshape(N_FF_CHUNKS, FF_CHUNK, D)
    return wgu, wd


def kernel(x, ln_mix_g, ln_mix_b, ln_ffn_g, ln_ffn_b, pool_w, pool_scale, w_qkv, w_o, w_gate, w_up, w_down):
    B, S, D = x.shape
    assert D == D_MODEL and S % TOKEN_TILE == 0 and S % ATT_BLOCK == 0
    row = lambda p, i: p[i].reshape(1, D)

    h = _pool_ln(x, pool_w[0].astype(BF16), row(pool_scale, 0), row(ln_mix_g, 0), row(ln_mix_b, 0))
    h = h.reshape(B * S, D)
    h = _ffn_ln(h, *_ffn_weights(w_gate[0], w_up[0], w_down[0]), row(ln_ffn_g, 0), row(ln_ffn_b, 0))

    qkv = _qkv(h, w_qkv[0].astype(BF16)).reshape(B, S, 3 * D)
    o = _attention(qkv).reshape(B * S, D)
    h = _proj_ln(o, h, w_o[0].astype(BF16), row(ln_mix_g, 1), row(ln_mix_b, 1))
    h = _ffn_ln(h, *_ffn_weights(w_gate[1], w_up[1], w_down[1]), row(ln_ffn_g, 1), row(ln_ffn_b, 1))
    return h.reshape(B, S, D)
```

```python
import jax
import jax.numpy as jnp
from jax import lax
from jax.experimental import pallas as pl
from jax.experimental.pallas import tpu as pltpu

D_MODEL = 1024
DEPTH = 2
POOL_WINDOWS = (2, 4, 8, 16)
N_POOL_GROUPS = len(POOL_WINDOWS)
POOL_GROUP_W = D_MODEL // N_POOL_GROUPS
N_HEADS = 16
HEAD_DIM = D_MODEL // N_HEADS
D_FF = 2816
DEEPNORM_ALPHA = (2.0 * DEPTH) ** 0.25
LN_EPS = 1e-5

F32 = jnp.float32
BF16 = jnp.bfloat16

LANES = 128
VMEM_LIMIT_BYTES = 56 * 1024 * 1024

TOKEN_TILE = 512
POOL_HALO = 16
FF_CHUNK = 256
N_FF_CHUNKS = D_FF // FF_CHUNK
QKV_CHUNK = 512

HEADS_PER_SLAB = LANES // HEAD_DIM
N_SLABS = N_HEADS // HEADS_PER_SLAB
ATT_BLOCK = 128
EXP_UNDERFLOW = 104.0
FAST_BLOCKS = 3
FAST_QUERY_BLOCKS = 2


def _layer_norm(v, g, b):
    mu = jnp.mean(v, axis=-1, keepdims=True)
    c = v - mu
    var = jnp.mean(c * c, axis=-1, keepdims=True)
    return c * lax.rsqrt(var + LN_EPS) * g + b


def _resident(shape):
    nd = len(shape)
    return pl.BlockSpec(shape, lambda *_: (0,) * nd, pipeline_mode=pl.Buffered(1))


def _params(*sem):
    return pltpu.CompilerParams(dimension_semantics=sem, vmem_limit_bytes=VMEM_LIMIT_BYTES)


def _pool_ln_kernel(x_ref, halo_ref, w_ref, scale_ref, g_ref, b_ref, o_ref):
    i = pl.program_id(1)
    x = x_ref[0]
    halo = jnp.where(i > 0, halo_ref[0], 0.0)
    t = i * TOKEN_TILE + lax.broadcasted_iota(jnp.int32, (TOKEN_TILE, 1), 0)
    ys = []
    for gi, w in enumerate(POOL_WINDOWS):
        cols = slice(gi * POOL_GROUP_W, (gi + 1) * POOL_GROUP_W)
        xg = x[:, cols]
        s = jnp.concatenate([halo[:, cols], xg], axis=0)
        k = 1
        while k < w:
            s = s + pltpu.roll(s, k, axis=0)
            k *= 2
        cnt = jnp.minimum(t + 1, w).astype(F32)
        mix = s[POOL_HALO:, :] / cnt - xg
        ys.append(jnp.dot(mix.astype(BF16), w_ref[gi], preferred_element_type=F32))
    y = jnp.concatenate(ys, axis=1) * scale_ref[...]
    o_ref[0] = _layer_norm(DEEPNORM_ALPHA * x + y, g_ref[...], b_ref[...])


def _pool_ln(x, w_grp, scale, g, b):
    B, S, D = x.shape
    halo_blocks_per_tile = TOKEN_TILE // POOL_HALO
    return pl.pallas_call(
        _pool_ln_kernel,
        grid=(B, S // TOKEN_TILE),
        in_specs=[
            pl.BlockSpec((1, TOKEN_TILE, D), lambda bi, i: (bi, i, 0)),
            pl.BlockSpec((1, POOL_HALO, D),
                         lambda bi, i: (bi, jnp.maximum(i * halo_blocks_per_tile - 1, 0), 0)),
            _resident(w_grp.shape), _resident(scale.shape), _resident(g.shape), _resident(b.shape),
        ],
        out_specs=pl.BlockSpec((1, TOKEN_TILE, D), lambda bi, i: (bi, i, 0)),
        out_shape=jax.ShapeDtypeStruct(x.shape, F32),
        compiler_params=_params("parallel", "parallel"),
        name="pool_ln",
    )(x, x, w_grp, scale, g, b)


def _ffn_ln_kernel(x_ref, wgu_ref, wd_ref, g_ref, b_ref, o_ref):
    x = x_ref[...]
    xb = x.astype(BF16)
    acc = None
    for c in range(N_FF_CHUNKS):
        gu = jnp.dot(xb, wgu_ref[c], preferred_element_type=F32)
        gate, up = gu[:, :FF_CHUNK], gu[:, FF_CHUNK:]
        h = (gate * jax.nn.sigmoid(gate) * up).astype(BF16)
        p = jnp.dot(h, wd_ref[c], preferred_element_type=F32)
        acc = p if acc is None else acc + p
    o_ref[...] = _layer_norm(DEEPNORM_ALPHA * x + acc, g_ref[...], b_ref[...])


def _ffn_ln(x2d, wgu, wd, g, b):
    T, D = x2d.shape
    row = pl.BlockSpec((TOKEN_TILE, D), lambda i: (i, 0))
    return pl.pallas_call(
        _ffn_ln_kernel,
        grid=(T // TOKEN_TILE,),
        in_specs=[row, _resident(wgu.shape), _resident(wd.shape), _resident(g.shape), _resident(b.shape)],
        out_specs=row,
        out_shape=jax.ShapeDtypeStruct(x2d.shape, F32),
        compiler_params=_params("parallel"),
        name="ffn_ln",
    )(x2d, wgu, wd, g, b)


def _qkv_kernel(x_ref, w_ref, o_ref):
    xb = x_ref[...].astype(BF16)
    for c in range(3 * D_MODEL // QKV_CHUNK):
        cols = slice(c * QKV_CHUNK, (c + 1) * QKV_CHUNK)
        y = jnp.dot(xb, w_ref[:, cols], preferred_element_type=F32)
        if (c + 1) * QKV_CHUNK <= D_MODEL:
            y = y * (HEAD_DIM ** -0.5)
        o_ref[:, cols] = y.astype(BF16)


def _qkv(x2d, w_qkv):
    T, D = x2d.shape
    return pl.pallas_call(
        _qkv_kernel,
        grid=(T // TOKEN_TILE,),
        in_specs=[pl.BlockSpec((TOKEN_TILE, D), lambda i: (i, 0)), _resident(w_qkv.shape)],
        out_specs=pl.BlockSpec((TOKEN_TILE, 3 * D), lambda i: (i, 0)),
        out_shape=jax.ShapeDtypeStruct((T, 3 * D), BF16),
        compiler_params=_params("parallel"),
        name="qkv_proj",
    )(x2d, w_qkv)


def _attn_kernel(q_ref, k_ref, v_ref, o_ref, acc_ref, carry_ref):
    n_blocks = q_ref.shape[1] // ATT_BLOCK
    lane = lax.broadcasted_iota(jnp.int32, (ATT_BLOCK, LANES), 1)
    first_head = lane < HEAD_DIM
    r2 = lax.broadcasted_iota(jnp.int32, (2 * ATT_BLOCK, 2 * ATT_BLOCK), 0) % ATT_BLOCK
    c2 = lax.broadcasted_iota(jnp.int32, (2 * ATT_BLOCK, 2 * ATT_BLOCK), 1)
    m2 = jnp.where((c2 >= ATT_BLOCK) | (r2 > c2), 1.0, 0.0).astype(BF16)
    qi = lax.broadcasted_iota(jnp.int32, (ATT_BLOCK, ATT_BLOCK), 0)
    ki = lax.broadcasted_iota(jnp.int32, (ATT_BLOCK, ATT_BLOCK), 1)
    causal = ki < qi
    nt_dims = (((1,), (1,)), ((), ()))

    def block_rows(j, n=1):
        return pl.ds(pl.multiple_of(j * ATT_BLOCK, ATT_BLOCK), n * ATT_BLOCK)

    def head_queries(i):
        q = q_ref[0, block_rows(i), :]
        zero = jnp.zeros_like(q)
        return jnp.where(first_head, q, zero), jnp.where(first_head, zero, q)

    def split(z, diag):
        sp = jnp.maximum(z, 0.0) + jnp.log(1.0 + jnp.exp(-jnp.abs(z)))
        log_beta = z - sp
        if diag:
            sp = jnp.where(causal, sp, 0.0)
        hi = sp.astype(BF16)
        lo = (sp - hi.astype(F32)).astype(BF16)
        return log_beta, jnp.concatenate([hi, lo], axis=1)

    def sweep(j, qh, slot, h, diag):
        z = lax.dot_general(qh, k_ref[0, block_rows(j), :], nt_dims, preferred_element_type=F32)
        log_beta, parts = split(z, diag)
        st = jnp.dot(parts, m2, preferred_element_type=F32)
        suffix, total = st[:, :ATT_BLOCK], st[:, ATT_BLOCK:]
        v = v_ref[0, block_rows(j), :]
        if diag:
            a = jnp.where(causal, jnp.exp(log_beta - suffix), 0.0)
            acc_ref[slot, h] = jnp.dot(a.astype(BF16), v, preferred_element_type=F32)
            carry = total
        else:
            carry = carry_ref[slot, h]
            a = jnp.exp(log_beta - suffix - carry)
            acc_ref[slot, h] += jnp.dot(a.astype(BF16), v, preferred_element_type=F32)
            carry = carry + total
        carry_ref[slot, h] = carry
        return jnp.min(carry)

    def finish(i, slot, done, least):
        def more(state):
            j, least = state
            return jnp.logical_and(j >= 0, least <= EXP_UNDERFLOW)

        def step(state):
            j, _ = state
            qa, qb = head_queries(i)
            return j - 1, jnp.minimum(sweep(j, qa, slot, 0, False), sweep(j, qb, slot, 1, False))

        lax.while_loop(more, step, (i - done, least))
        o_ref[0, block_rows(i), :] = jnp.where(first_head, acc_ref[slot, 0], acc_ref[slot, 1]).astype(BF16)

    def slow_block(i, _):
        qa, qb = head_queries(i)
        least = jnp.minimum(sweep(i, qa, 0, 0, True), sweep(i, qb, 0, 1, True))
        finish(i, 0, 1, least)
        return 0

    def fast_group(g, _):
        nb = FAST_BLOCKS
        i0 = FAST_BLOCKS - 1 + g * FAST_QUERY_BLOCKS
        chains = [(slot, h) for slot in range(FAST_QUERY_BLOCKS) for h in range(HEADS_PER_SLAB)]
        queries = [head_queries(i0 + slot) for slot in range(FAST_QUERY_BLOCKS)]
        key_rows = [block_rows(i0 + slot - (nb - 1), nb) for slot in range(FAST_QUERY_BLOCKS)]
        zs = [lax.dot_general(queries[slot][h], k_ref[0, key_rows[slot], :], nt_dims,
                              preferred_element_type=F32) for slot, h in chains]
        halves = [[split(z[:, b * ATT_BLOCK:(b + 1) * ATT_BLOCK], b == nb - 1) for b in range(nb)]
                  for z in zs]
        sts = [jnp.dot(jnp.concatenate([parts for _, parts in half], axis=0), m2,
                       preferred_element_type=F32) for half in halves]
        outs = []
        for half, st in zip(halves, sts):
            carry = None
            a = [None] * nb
            for b in reversed(range(nb)):
                sb = st[b * ATT_BLOCK:(b + 1) * ATT_BLOCK]
                suffix, total = sb[:, :ATT_BLOCK], sb[:, ATT_BLOCK:]
                log_beta = half[b][0]
                if carry is None:
                    a[b] = jnp.where(causal, jnp.exp(log_beta - suffix), 0.0).astype(BF16)
                    carry = total
                else:
                    a[b] = jnp.exp(log_beta - suffix - carry).astype(BF16)
                    carry = carry + total
            outs.append((jnp.concatenate(a, axis=1), carry))
        least = [None] * FAST_QUERY_BLOCKS
        for (slot, h), (a, carry) in zip(chains, outs):
            acc_ref[slot, h] = jnp.dot(a, v_ref[0, key_rows[slot], :], preferred_element_type=F32)
            carry_ref[slot, h] = carry
            m = jnp.min(carry)
            least[slot] = m if least[slot] is None else jnp.minimum(least[slot], m)
        for slot in range(FAST_QUERY_BLOCKS):
            finish(i0 + slot, slot, nb, least[slot])
        return 0

    n_fast_groups = (n_blocks - (FAST_BLOCKS - 1)) // FAST_QUERY_BLOCKS
    lax.fori_loop(0, FAST_BLOCKS - 1, slow_block, 0)
    lax.fori_loop(0, n_fast_groups, fast_group, 0)
    lax.fori_loop(FAST_BLOCKS - 1 + n_fast_groups * FAST_QUERY_BLOCKS, n_blocks, slow_block, 0)


def _attention(qkv):
    B, S, _ = qkv.shape
    def slab(which):
        return pl.BlockSpec((1, S, LANES), lambda bi, p: (bi, 0, which * N_SLABS + p))
    return pl.pallas_call(
        _attn_kernel,
        grid=(B, N_SLABS),
        in_specs=[slab(0), slab(1), slab(2)],
        out_specs=pl.BlockSpec((1, S, LANES), lambda bi, p: (bi, 0, p)),
        out_shape=jax.ShapeDtypeStruct((B, S, D_MODEL), BF16),
        scratch_shapes=[pltpu.VMEM((FAST_QUERY_BLOCKS, HEADS_PER_SLAB, ATT_BLOCK, LANES), F32),
                        pltpu.VMEM((FAST_QUERY_BLOCKS, HEADS_PER_SLAB, ATT_BLOCK, ATT_BLOCK), F32)],
        compiler_params=_params("parallel", "parallel"),
        name="stickbreak_attn",
    )(qkv, qkv, qkv)


def _proj_ln_kernel(o_ref, x_ref, w_ref, g_ref, b_ref, y_ref):
    m = jnp.dot(o_ref[...], w_ref[...], preferred_element_type=F32)
    y_ref[...] = _layer_norm(DEEPNORM_ALPHA * x_ref[...] + m, g_ref[...], b_ref[...])


def _proj_ln(o2d, x2d, w_o, g, b):
    T, D = x2d.shape
    row = pl.BlockSpec((TOKEN_TILE, D), lambda i: (i, 0))
    return pl.pallas_call(
        _proj_ln_kernel,
        grid=(T // TOKEN_TILE,),
        in_specs=[row, row, _resident(w_o.shape), _resident(g.shape), _resident(b.shape)],
        out_specs=row,
        out_shape=jax.ShapeDtypeStruct(x2d.shape, F32),
        compiler_params=_params("parallel"),
        name="proj_ln",
    )(o2d, x2d, w_o, g, b)


def _ffn_weights(w_gate, w_up, w_down):
    D = w_gate.shape[0]
    wg = w_gate.astype(BF16).reshape(D, N_FF_CHUNKS, FF_CHUNK)
    wu = w_up.astype(BF16).reshape(D, N_FF_CHUNKS, FF_CHUNK)
    wgu = jnp.transpose(jnp.concatenate([wg, wu], axis=2), (1, 0, 2))
    wd = w_down.astype(BF16).reshape(N_FF_CHUNKS, FF_CHUNK, D)
    return wgu, wd


def kernel(x, ln_mix_g, ln_mix_b, ln_ffn_g, ln_ffn_b, pool_w, pool_scale, w_qkv, w_o, w_gate, w_up, w_down):
    B, S, D = x.shape
    assert D == D_MODEL and S % TOKEN_TILE == 0 and S % ATT_BLOCK == 0
    row = lambda p, i: p[i].reshape(1, D)

    h = _pool_ln(x, pool_w[0].astype(BF16), row(pool_scale, 0), row(ln_mix_g, 0), row(ln_mix_b, 0))
    h = h.reshape(B * S, D)
    h = _ffn_ln(h, *_ffn_weights(w_gate[0], w_up[0], w_down[0]), row(ln_ffn_g, 0), row(ln_ffn_b, 0))

    qkv = _qkv(h, w_qkv[0].astype(BF16)).reshape(B, S, 3 * D)
    o = _attention(qkv).reshape(B * S, D)
    h = _proj_ln(o, h, w_o[0].astype(BF16), row(ln_mix_g, 1), row(ln_mix_b, 1))
    h = _ffn_ln(h, *_ffn_weights(w_gate[1], w_up[1], w_down[1]), row(ln_ffn_g, 1), row(ln_ffn_b, 1))
    return h.reshape(B, S, D)
```

```python
import jax
import jax.numpy as jnp
from jax import lax
from jax.experimental import pallas as pl
from jax.experimental.pallas import tpu as pltpu

D_MODEL = 1024
DEPTH = 2
POOL_WINDOWS = (2, 4, 8, 16)
N_POOL_GROUPS = len(POOL_WINDOWS)
POOL_GROUP_W = D_MODEL // N_POOL_GROUPS
N_HEADS = 16
HEAD_DIM = D_MODEL // N_HEADS
D_FF = 2816
DEEPNORM_ALPHA = (2.0 * DEPTH) ** 0.25
LN_EPS = 1e-5

F32 = jnp.float32
BF16 = jnp.bfloat16

LANES = 128
VMEM_LIMIT_BYTES = 56 * 1024 * 1024

TOKEN_TILE = 512
POOL_HALO = 16
FF_CHUNK = 256
assert D_FF % FF_CHUNK == 0
N_FF_CHUNKS = D_FF // FF_CHUNK
QKV_CHUNK = 512

HEADS_PER_SLAB = LANES // HEAD_DIM
N_SLABS = N_HEADS // HEADS_PER_SLAB
Q_ROWS = 64
KEY_BLOCK = 128
WINDOW_BLOCKS = 2
FAST_WINDOW = WINDOW_BLOCKS * KEY_BLOCK
GROUP_BLOCKS = 4
EXP_UNDERFLOW = 104.0
SIGN_BIT = 0x80000000


def _layer_norm(v, g, b):
    mu = jnp.mean(v, axis=-1, keepdims=True)
    c = v - mu
    var = jnp.mean(c * c, axis=-1, keepdims=True)
    return c * lax.rsqrt(var + LN_EPS) * g + b


def _resident(shape):
    nd = len(shape)
    return pl.BlockSpec(shape, lambda *_: (0,) * nd, pipeline_mode=pl.Buffered(1))


def _params(*sem):
    return pltpu.CompilerParams(dimension_semantics=sem, vmem_limit_bytes=VMEM_LIMIT_BYTES)


def _pool_mixer(x, halo, t0, w_ref, scale):
    t = t0 + lax.broadcasted_iota(jnp.int32, (TOKEN_TILE, 1), 0)
    ys = []
    for gi, w in enumerate(POOL_WINDOWS):
        cols = slice(gi * POOL_GROUP_W, (gi + 1) * POOL_GROUP_W)
        xg = x[:, cols]
        s = jnp.concatenate([halo[:, cols], xg], axis=0)
        k = 1
        while k < w:
            s = s + pltpu.roll(s, k, axis=0)
            k *= 2
        cnt = jnp.minimum(t + 1, w).astype(F32)
        mix = s[POOL_HALO:, :] / cnt - xg
        ys.append(jnp.dot(mix.astype(BF16), w_ref[gi], preferred_element_type=F32))
    return jnp.concatenate(ys, axis=1) * scale


def _swiglu(x, wg_ref, wu_ref, wd_ref):
    xb = x.astype(BF16)
    acc = None
    for c in range(N_FF_CHUNKS):
        cols = slice(c * FF_CHUNK, (c + 1) * FF_CHUNK)
        gate = jnp.dot(xb, wg_ref[:, cols], preferred_element_type=F32)
        up = jnp.dot(xb, wu_ref[:, cols], preferred_element_type=F32)
        h = (gate * jax.nn.sigmoid(gate) * up).astype(BF16)
        p = jnp.dot(h, wd_ref[cols, :], preferred_element_type=F32)
        acc = p if acc is None else acc + p
    return acc


def _ffn_sublayer(x, wg_ref, wu_ref, wd_ref, g, b):
    return _layer_norm(DEEPNORM_ALPHA * x + _swiglu(x, wg_ref, wu_ref, wd_ref), g, b)


def _layer0_kernel(x_ref, halo_ref, pw_ref, ps_ref, g1_ref, b1_ref, wg_ref, wu_ref, wd_ref, g2_ref, b2_ref,
                   wqkv_ref, y_ref, qkv_ref):
    i = pl.program_id(1)
    x = x_ref[0]
    halo = jnp.where(i > 0, halo_ref[0], 0.0)
    m = _pool_mixer(x, halo, i * TOKEN_TILE, pw_ref, ps_ref[...])
    h = _layer_norm(DEEPNORM_ALPHA * x + m, g1_ref[...], b1_ref[...])
    y = _ffn_sublayer(h, wg_ref, wu_ref, wd_ref, g2_ref[...], b2_ref[...])
    y_ref[0] = y
    yb = y.astype(BF16)
    for c in range(3 * D_MODEL // QKV_CHUNK):
        cols = slice(c * QKV_CHUNK, (c + 1) * QKV_CHUNK)
        p = jnp.dot(yb, wqkv_ref[:, cols], preferred_element_type=F32)
        if (c + 1) * QKV_CHUNK <= D_MODEL:
            p = p * (HEAD_DIM ** -0.5)
        qkv_ref[0, :, cols] = p.astype(BF16)


def _layer0(x, pool_w, pool_scale, g1, b1, wg, wu, wd, g2, b2, w_qkv):
    B, S, D = x.shape
    halo_blocks_per_tile = TOKEN_TILE // POOL_HALO
    tile = lambda width: pl.BlockSpec((1, TOKEN_TILE, width), lambda bi, i: (bi, i, 0))
    consts = (pool_w, pool_scale, g1, b1, wg, wu, wd, g2, b2, w_qkv)
    return pl.pallas_call(
        _layer0_kernel,
        grid=(B, S // TOKEN_TILE),
        in_specs=[tile(D),
                  pl.BlockSpec((1, POOL_HALO, D),
                               lambda bi, i: (bi, jnp.maximum(i * halo_blocks_per_tile - 1, 0), 0))]
                 + [_resident(c.shape) for c in consts],
        out_specs=[tile(D), tile(3 * D)],
        out_shape=[jax.ShapeDtypeStruct((B, S, D), F32), jax.ShapeDtypeStruct((B, S, 3 * D), BF16)],
        compiler_params=_params("parallel", "parallel"),
        name="layer0_qkv",
    )(x, x, *consts)


def _attn_kernel(q_ref, k_ref, v_ref, o_ref, acc_ref, carry_ref, z_ref, w_ref, tot_ref):
    seq = q_ref.shape[1]
    n_groups = seq // (Q_ROWS * GROUP_BLOCKS)
    rows2 = HEADS_PER_SLAB * Q_ROWS
    slots = range(GROUP_BLOCKS)
    lane = lax.broadcasted_iota(jnp.int32, (Q_ROWS, LANES), 1)
    first_head = lane < HEAD_DIM
    nt_dims = (((1,), (1,)), ((), ()))

    tri_r = lax.broadcasted_iota(jnp.int32, (2 * KEY_BLOCK, 2 * KEY_BLOCK), 0) % KEY_BLOCK
    tri_c = lax.broadcasted_iota(jnp.int32, (2 * KEY_BLOCK, 2 * KEY_BLOCK), 1)
    tri = jnp.where((tri_c >= KEY_BLOCK) | (tri_r >= tri_c), 1.0, 0.0).astype(BF16)
    col = lax.broadcasted_iota(jnp.int32, (rows2, KEY_BLOCK), 1)
    skew = col - lax.broadcasted_iota(jnp.int32, (rows2, KEY_BLOCK), 0) % Q_ROWS

    def query_rows(i):
        return pl.ds(pl.multiple_of(i * Q_ROWS, Q_ROWS), Q_ROWS)

    def stacked_queries(i):
        q = q_ref[0, query_rows(i), :]
        zero = jnp.zeros_like(q)
        return jnp.concatenate([jnp.where(first_head, q, zero), jnp.where(first_head, zero, q)], axis=0)

    def softplus(z):
        neg_abs = lax.bitcast_convert_type(
            lax.bitcast_convert_type(z, jnp.uint32) | jnp.uint32(SIGN_BIT), F32)
        return jnp.maximum(z, 0.0) + jnp.log(1.0 + jnp.exp(neg_abs))

    def hi_lo(sp):
        hi = sp.astype(BF16)
        lo = (sp - hi.astype(F32)).astype(BF16)
        return jnp.concatenate([hi, lo], axis=1)


    def sweep(i, slot, start_raw, limit):
        start = pl.multiple_of(jnp.maximum(start_raw, 0), Q_ROWS)
        rows = pl.ds(start, KEY_BLOCK)
        z = lax.dot_general(stacked_queries(i), k_ref[0, rows, :], nt_dims, preferred_element_type=F32)
        valid = jnp.logical_and(skew < i * Q_ROWS - start, col < limit - start)
        sp = jnp.where(valid, softplus(z), 0.0)
        st = jnp.dot(hi_lo(sp), tri, preferred_element_type=F32)
        carry = carry_ref[slot]
        a = jnp.where(valid, jnp.exp(z - st[:, :KEY_BLOCK] - carry), 0.0)
        acc_ref[slot] += jnp.dot(a.astype(BF16), v_ref[0, rows, :], preferred_element_type=F32)
        carry = carry + st[:, KEY_BLOCK:]
        carry_ref[slot] = carry
        return jnp.min(carry)

    def finish(i, slot, window_start, least):
        def more(state):
            start_raw, _, least = state
            return jnp.logical_and(start_raw > -KEY_BLOCK, least <= EXP_UNDERFLOW)

        def step(state):
            start_raw, limit, _ = state
            least = sweep(i, slot, start_raw, limit)
            return start_raw - KEY_BLOCK, jnp.maximum(start_raw, 0), least

        lax.while_loop(more, step, (window_start - KEY_BLOCK, window_start, least))
        acc = acc_ref[slot]
        o_ref[0, query_rows(i), :] = jnp.where(first_head, acc[:Q_ROWS], acc[Q_ROWS:]).astype(BF16)


    def window_start(i):
        return (i + 1) * Q_ROWS - FAST_WINDOW

    def window_rows(i):
        return pl.ds(pl.multiple_of(jnp.maximum(window_start(i), 0), Q_ROWS), FAST_WINDOW)

    def stage_scores(g, buf):
        for slot in slots:
            i = g * GROUP_BLOCKS + slot
            z_ref[buf, slot] = lax.dot_general(stacked_queries(i), k_ref[0, window_rows(i), :],
                                               nt_dims, preferred_element_type=F32)

    def stage_weights(buf, first_group):
        def visible(slot, b):
            i = slot if first_group else GROUP_BLOCKS
            bound = i * Q_ROWS - max(window_start(i), 0) - b * KEY_BLOCK
            return None if bound >= KEY_BLOCK else skew < bound

        zs, masks, parts = [], [], []
        for slot in slots:
            z = z_ref[buf, slot]
            for b in range(WINDOW_BLOCKS):
                zb = z[:, b * KEY_BLOCK:(b + 1) * KEY_BLOCK]
                m = visible(slot, b)
                sp = softplus(zb)
                zs.append(zb)
                masks.append(m)
                parts.append(hi_lo(sp if m is None else jnp.where(m, sp, 0.0)))
        st = jnp.dot(jnp.concatenate(parts, axis=0), tri, preferred_element_type=F32)
        for slot in slots:
            carry = None
            a = [None] * WINDOW_BLOCKS
            for b in reversed(range(WINDOW_BLOCKS)):
                n = slot * WINDOW_BLOCKS + b
                sb = st[n * rows2:(n + 1) * rows2]
                x = zs[n] - sb[:, :KEY_BLOCK]
                if carry is not None:
                    x = x - carry
                e = jnp.exp(x)
                a[b] = (e if masks[n] is None else jnp.where(masks[n], e, 0.0)).astype(BF16)
                carry = sb[:, KEY_BLOCK:] if carry is None else carry + sb[:, KEY_BLOCK:]
            w_ref[buf, slot] = jnp.concatenate(a, axis=1)
            tot_ref[buf, slot] = carry

    def stage_values(g, buf):
        least = []
        for slot in slots:
            i = g * GROUP_BLOCKS + slot
            acc_ref[slot] = jnp.dot(w_ref[buf, slot], v_ref[0, window_rows(i), :], preferred_element_type=F32)
            total = tot_ref[buf, slot]
            carry_ref[slot] = total
            least.append(jnp.min(total))
        return least

    def stage_finish(g, least):
        for slot in slots:
            i = g * GROUP_BLOCKS + slot
            finish(i, slot, window_start(i), least[slot])

    def step(g, buf):
        least = stage_values(g - 1, 1 - buf)
        stage_scores(g + 1, 1 - buf)
        stage_weights(buf, False)
        stage_finish(g - 1, least)

    def step_pair(p, _):
        step(1 + 2 * p, 1)
        step(2 + 2 * p, 0)
        return 0

    assert n_groups >= 2 and n_groups * GROUP_BLOCKS * Q_ROWS == seq
    assert window_start(GROUP_BLOCKS) >= 0
    stage_scores(0, 0)
    stage_scores(1, 1)
    stage_weights(0, True)
    last = n_groups - 1
    lax.fori_loop(0, (last - 1) // 2, step_pair, 0)
    if (last - 1) % 2:
        step(last - 1, (last - 1) % 2)
    least = stage_values(last - 1, (last - 1) % 2)
    stage_weights(last % 2, False)
    stage_finish(last - 1, least)
    stage_finish(last, stage_values(last, last % 2))


def _attention(qkv):
    B, S, _ = qkv.shape
    rows2 = HEADS_PER_SLAB * Q_ROWS
    def slab(which):
        return pl.BlockSpec((1, S, LANES), lambda bi, p: (bi, 0, which * N_SLABS + p))
    return pl.pallas_call(
        _attn_kernel,
        grid=(B, N_SLABS),
        in_specs=[slab(0), slab(1), slab(2)],
        out_specs=pl.BlockSpec((1, S, LANES), lambda bi, p: (bi, 0, p)),
        out_shape=jax.ShapeDtypeStruct((B, S, D_MODEL), BF16),
        scratch_shapes=[pltpu.VMEM((GROUP_BLOCKS, rows2, LANES), F32),
                        pltpu.VMEM((GROUP_BLOCKS, rows2, KEY_BLOCK), F32),
                        pltpu.VMEM((2, GROUP_BLOCKS, rows2, FAST_WINDOW), F32),
                        pltpu.VMEM((2, GROUP_BLOCKS, rows2, FAST_WINDOW), BF16),
                        pltpu.VMEM((2, GROUP_BLOCKS, rows2, KEY_BLOCK), F32)],
        compiler_params=_params("parallel", "parallel"),
        name="stickbreak_attn",
    )(qkv, qkv, qkv)


def _layer1_kernel(o_ref, x_ref, wo_ref, g1_ref, b1_ref, wg_ref, wu_ref, wd_ref, g2_ref, b2_ref, y_ref):
    m = jnp.dot(o_ref[...], wo_ref[...], preferred_element_type=F32)
    h = _layer_norm(DEEPNORM_ALPHA * x_ref[...] + m, g1_ref[...], b1_ref[...])
    y_ref[...] = _ffn_sublayer(h, wg_ref, wu_ref, wd_ref, g2_ref[...], b2_ref[...])


def _layer1(o2d, x2d, w_o, g1, b1, wg, wu, wd, g2, b2):
    T, D = x2d.shape
    row = pl.BlockSpec((TOKEN_TILE, D), lambda i: (i, 0))
    consts = (w_o, g1, b1, wg, wu, wd, g2, b2)
    return pl.pallas_call(
        _layer1_kernel,
        grid=(T // TOKEN_TILE,),
        in_specs=[row, row] + [_resident(c.shape) for c in consts],
        out_specs=row,
        out_shape=jax.ShapeDtypeStruct(x2d.shape, F32),
        compiler_params=_params("parallel"),
        name="proj_ffn",
    )(o2d, x2d, *consts)


def kernel(x, ln_mix_g, ln_mix_b, ln_ffn_g, ln_ffn_b, pool_w, pool_scale, w_qkv, w_o, w_gate, w_up, w_down):
    B, S, D = x.shape
    assert D == D_MODEL and S % TOKEN_TILE == 0 and S % (2 * GROUP_BLOCKS * Q_ROWS) == 0
    row = lambda p, i: p[i].reshape(1, D)
    bf = lambda w: w.astype(BF16)

    h, qkv = _layer0(x, bf(pool_w[0]), row(pool_scale, 0), row(ln_mix_g, 0), row(ln_mix_b, 0),
                     bf(w_gate[0]), bf(w_up[0]), bf(w_down[0]), row(ln_ffn_g, 0), row(ln_ffn_b, 0),
                     bf(w_qkv[0]))
    o = _attention(qkv)
    y = _layer1(o.reshape(B * S, D), h.reshape(B * S, D), bf(w_o[0]), row(ln_mix_g, 1), row(ln_mix_b, 1),
                bf(w_gate[1]), bf(w_up[1]), bf(w_down[1]), row(ln_ffn_g, 1), row(ln_ffn_b, 1))
    return y.reshape(B, S, D)
```

```python
import jax
import jax.numpy as jnp
from jax import lax
from jax.experimental import pallas as pl
from jax.experimental.pallas import tpu as pltpu

D_MODEL = 1024
DEPTH = 2
POOL_WINDOWS = (2, 4, 8, 16)
N_POOL_GROUPS = len(POOL_WINDOWS)
POOL_GROUP_W = D_MODEL // N_POOL_GROUPS
N_HEADS = 16
HEAD_DIM = D_MODEL // N_HEADS
D_FF = 2816
DEEPNORM_ALPHA = (2.0 * DEPTH) ** 0.25
LN_EPS = 1e-5

F32 = jnp.float32
BF16 = jnp.bfloat16

LANES = 128
VMEM_LIMIT_BYTES = 56 * 1024 * 1024

PART_ROWS = 512
LAYER0_PARTS = 1
LAYER1_PARTS = 2
POOL_HALO = 16
FF_CHUNK = 256
assert D_FF % FF_CHUNK == 0
N_FF_CHUNKS = D_FF // FF_CHUNK
QKV_CHUNK = 512

HEADS_PER_SLAB = LANES // HEAD_DIM
N_SLABS = N_HEADS // HEADS_PER_SLAB
Q_ROWS = 64
KEY_BLOCK = 128
WINDOW_BLOCKS = 2
FAST_WINDOW = WINDOW_BLOCKS * KEY_BLOCK
GROUP_BLOCKS = 4
EXP_UNDERFLOW = 104.0


def _layer_norm(v, g, b):
    mu = jnp.mean(v, axis=-1, keepdims=True)
    c = v - mu
    var = jnp.mean(c * c, axis=-1, keepdims=True)
    return c * lax.rsqrt(var + LN_EPS) * g + b


def _resident(shape):
    nd = len(shape)
    return pl.BlockSpec(shape, lambda *_: (0,) * nd, pipeline_mode=pl.Buffered(1))


def _params(*sem):
    return pltpu.CompilerParams(dimension_semantics=sem, vmem_limit_bytes=VMEM_LIMIT_BYTES)


def _pool_mixer(x, halo, t0, w_ref, scale):
    rows = x.shape[0]
    t = t0 + lax.broadcasted_iota(jnp.int32, (rows, 1), 0)
    ys = []
    for gi, w in enumerate(POOL_WINDOWS):
        cols = slice(gi * POOL_GROUP_W, (gi + 1) * POOL_GROUP_W)
        xg = x[:, cols]
        s = jnp.concatenate([halo[:, cols], xg], axis=0)
        k = 1
        while k < w:
            s = s + pltpu.roll(s, k, axis=0)
            k *= 2
        cnt = jnp.minimum(t + 1, w).astype(F32)
        mix = s[POOL_HALO:, :] / cnt - xg
        ys.append(jnp.dot(mix.astype(BF16), w_ref[gi], preferred_element_type=F32))
    return jnp.concatenate(ys, axis=1) * scale


def _swiglu(x, wg_ref, wu_ref, wd_ref):
    xb = x.astype(BF16)
    acc = None
    for c in range(N_FF_CHUNKS):
        cols = slice(c * FF_CHUNK, (c + 1) * FF_CHUNK)
        gate = jnp.dot(xb, wg_ref[:, cols], preferred_element_type=F32)
        up = jnp.dot(xb, wu_ref[:, cols], preferred_element_type=F32)
        h = (gate * jax.nn.sigmoid(gate) * up).astype(BF16)
        p = jnp.dot(h, wd_ref[cols, :], preferred_element_type=F32)
        acc = p if acc is None else acc + p
    return acc


def _ffn_sublayers(hs, wg_ref, wu_ref, wd_ref, g, b):
    fs = [_swiglu(h, wg_ref, wu_ref, wd_ref) for h in hs]
    return [_layer_norm(DEEPNORM_ALPHA * h + f, g, b) for h, f in zip(hs, fs)]


def _part_rows(p):
    return slice(p * PART_ROWS, (p + 1) * PART_ROWS)


def _layer0_kernel(x_ref, halo_ref, pw_ref, ps_ref, g1_ref, b1_ref, wg_ref, wu_ref, wd_ref, g2_ref, b2_ref,
                   wqkv_ref, y_ref, qkv_ref):
    i = pl.program_id(1)
    xs = [x_ref[0, _part_rows(p), :] for p in range(LAYER0_PARTS)]
    first_halo = jnp.where(i > 0, halo_ref[0], 0.0)
    halos = [first_halo] + [x[PART_ROWS - POOL_HALO:, :] for x in xs[:-1]]
    ms = [_pool_mixer(x, halo, (i * LAYER0_PARTS + p) * PART_ROWS, pw_ref, ps_ref[...])
          for p, (x, halo) in enumerate(zip(xs, halos))]
    hs = [_layer_norm(DEEPNORM_ALPHA * x + m, g1_ref[...], b1_ref[...]) for x, m in zip(xs, ms)]
    ys = _ffn_sublayers(hs, wg_ref, wu_ref, wd_ref, g2_ref[...], b2_ref[...])
    for p, y in enumerate(ys):
        y_ref[0, _part_rows(p), :] = y
        yb = y.astype(BF16)
        for c in range(3 * D_MODEL // QKV_CHUNK):
            cols = slice(c * QKV_CHUNK, (c + 1) * QKV_CHUNK)
            out = jnp.dot(yb, wqkv_ref[:, cols], preferred_element_type=F32)
            if (c + 1) * QKV_CHUNK <= D_MODEL:
                out = out * (HEAD_DIM ** -0.5)
            out = out.astype(BF16)
            for k in range(QKV_CHUNK // LANES):
                qkv_ref[0, c * (QKV_CHUNK // LANES) + k, _part_rows(p), :] = out[:, k * LANES:(k + 1) * LANES]


def _layer0(x, pool_w, pool_scale, g1, b1, wg, wu, wd, g2, b2, w_qkv):
    B, S, D = x.shape
    tile = LAYER0_PARTS * PART_ROWS
    halo_blocks_per_tile = tile // POOL_HALO
    consts = (pool_w, pool_scale, g1, b1, wg, wu, wd, g2, b2, w_qkv)
    return pl.pallas_call(
        _layer0_kernel,
        grid=(B, S // tile),
        in_specs=[pl.BlockSpec((1, tile, D), lambda bi, i: (bi, i, 0)),
                  pl.BlockSpec((1, POOL_HALO, D),
                               lambda bi, i: (bi, jnp.maximum(i * halo_blocks_per_tile - 1, 0), 0))]
                 + [_resident(c.shape) for c in consts],
        out_specs=[pl.BlockSpec((1, tile, D), lambda bi, i: (bi, i, 0)),
                   pl.BlockSpec((1, 3 * N_SLABS, tile, LANES), lambda bi, i: (bi, 0, i, 0))],
        out_shape=[jax.ShapeDtypeStruct((B, S, D), F32),
                   jax.ShapeDtypeStruct((B, 3 * N_SLABS, S, LANES), BF16)],
        compiler_params=_params("parallel", "parallel"),
        name="layer0_qkv",
    )(x, x, *consts)


def _attn_kernel(q_ref, k_ref, v_ref, o_ref, acc_ref, carry_ref, z_ref, w_ref, tot_ref):
    seq = q_ref.shape[2]
    n_groups = seq // (Q_ROWS * GROUP_BLOCKS)
    rows2 = HEADS_PER_SLAB * Q_ROWS
    slots = range(GROUP_BLOCKS)
    lane = lax.broadcasted_iota(jnp.int32, (Q_ROWS, LANES), 1)
    first_head = lane < HEAD_DIM
    nt_dims = (((1,), (1,)), ((), ()))

    tri_r = lax.broadcasted_iota(jnp.int32, (2 * KEY_BLOCK, 2 * KEY_BLOCK), 0) % KEY_BLOCK
    tri_c = lax.broadcasted_iota(jnp.int32, (2 * KEY_BLOCK, 2 * KEY_BLOCK), 1)
    tri = jnp.where((tri_c >= KEY_BLOCK) | (tri_r >= tri_c), 1.0, 0.0).astype(BF16)
    col = lax.broadcasted_iota(jnp.int32, (rows2, KEY_BLOCK), 1)
    skew = col - lax.broadcasted_iota(jnp.int32, (rows2, KEY_BLOCK), 0) % Q_ROWS

    def query_rows(i):
        return pl.ds(pl.multiple_of(i * Q_ROWS, Q_ROWS), Q_ROWS)

    def stacked_queries(i):
        q = q_ref[0, 0, query_rows(i), :]
        zero = jnp.zeros_like(q)
        return jnp.concatenate([jnp.where(first_head, q, zero), jnp.where(first_head, zero, q)], axis=0)

    def softplus(z):
        return jnp.maximum(z, 0.0) + jnp.log(1.0 + jnp.exp(-jnp.abs(z)))

    def hi_lo(sp):
        hi = sp.astype(BF16)
        lo = (sp - hi.astype(F32)).astype(BF16)
        return jnp.concatenate([hi, lo], axis=1)


    def sweep(i, slot, start_raw, limit):
        start = pl.multiple_of(jnp.maximum(start_raw, 0), Q_ROWS)
        rows = pl.ds(start, KEY_BLOCK)
        z = lax.dot_general(stacked_queries(i), k_ref[0, 0, rows, :], nt_dims, preferred_element_type=F32)
        valid = jnp.logical_and(skew < i * Q_ROWS - start, col < limit - start)
        sp = jnp.where(valid, softplus(z), 0.0)
        st = jnp.dot(hi_lo(sp), tri, preferred_element_type=F32)
        carry = carry_ref[slot]
        a = jnp.where(valid, jnp.exp(z - st[:, :KEY_BLOCK] - carry), 0.0)
        acc_ref[slot] += jnp.dot(a.astype(BF16), v_ref[0, 0, rows, :], preferred_element_type=F32)
        carry = carry + st[:, KEY_BLOCK:]
        carry_ref[slot] = carry
        return jnp.min(carry)

    def finish(i, slot, window_start, least):
        def more(state):
            start_raw, _, least = state
            return jnp.logical_and(start_raw > -KEY_BLOCK, least <= EXP_UNDERFLOW)

        def step(state):
            start_raw, limit, _ = state
            least = sweep(i, slot, start_raw, limit)
            return start_raw - KEY_BLOCK, jnp.maximum(start_raw, 0), least

        lax.while_loop(more, step, (window_start - KEY_BLOCK, window_start, least))
        acc = acc_ref[slot]
        o_ref[0, 0, query_rows(i), :] = jnp.where(first_head, acc[:Q_ROWS], acc[Q_ROWS:]).astype(BF16)


    def window_start(i):
        return (i + 1) * Q_ROWS - FAST_WINDOW

    def window_rows(i):
        return pl.ds(pl.multiple_of(jnp.maximum(window_start(i), 0), Q_ROWS), FAST_WINDOW)

    def stage_scores(g, buf):
        for slot in slots:
            i = g * GROUP_BLOCKS + slot
            z_ref[buf, slot] = lax.dot_general(stacked_queries(i), k_ref[0, 0, window_rows(i), :],
                                               nt_dims, preferred_element_type=F32)

    def stage_weights(buf, first_group):
        def visible(slot, b):
            i = slot if first_group else GROUP_BLOCKS
            bound = i * Q_ROWS - max(window_start(i), 0) - b * KEY_BLOCK
            return None if bound >= KEY_BLOCK else skew < bound

        zs, masks, parts = [], [], []
        for slot in slots:
            z = z_ref[buf, slot]
            for b in range(WINDOW_BLOCKS):
                zb = z[:, b * KEY_BLOCK:(b + 1) * KEY_BLOCK]
                m = visible(slot, b)
                sp = softplus(zb)
                zs.append(zb)
                masks.append(m)
                parts.append(hi_lo(sp if m is None else jnp.where(m, sp, 0.0)))
        st = jnp.dot(jnp.concatenate(parts, axis=0), tri, preferred_element_type=F32)
        for slot in slots:
            carry = None
            a = [None] * WINDOW_BLOCKS
            for b in reversed(range(WINDOW_BLOCKS)):
                n = slot * WINDOW_BLOCKS + b
                sb = st[n * rows2:(n + 1) * rows2]
                x = zs[n] - sb[:, :KEY_BLOCK]
                if carry is not None:
                    x = x - carry
                e = jnp.exp(x)
                a[b] = (e if masks[n] is None else jnp.where(masks[n], e, 0.0)).astype(BF16)
                carry = sb[:, KEY_BLOCK:] if carry is None else carry + sb[:, KEY_BLOCK:]
            w_ref[buf, slot] = jnp.concatenate(a, axis=1)
            tot_ref[buf, slot] = carry

    def stage_values(g, buf):
        least = []
        for slot in slots:
            i = g * GROUP_BLOCKS + slot
            acc_ref[slot] = jnp.dot(w_ref[buf, slot], v_ref[0, 0, window_rows(i), :], preferred_element_type=F32)
            total = tot_ref[buf, slot]
            carry_ref[slot] = total
            least.append(jnp.min(total))
        return least

    def stage_finish(g, least):
        for slot in slots:
            i = g * GROUP_BLOCKS + slot
            finish(i, slot, window_start(i), least[slot])

    def step(g, buf):
        least = stage_values(g - 1, 1 - buf)
        stage_scores(g + 1, 1 - buf)
        stage_weights(buf, False)
        stage_finish(g - 1, least)

    def step_pair(p, _):
        step(1 + 2 * p, 1)
        step(2 + 2 * p, 0)
        return 0

    assert n_groups >= 2 and n_groups * GROUP_BLOCKS * Q_ROWS == seq
    assert window_start(GROUP_BLOCKS) >= 0
    stage_scores(0, 0)
    stage_scores(1, 1)
    stage_weights(0, True)
    last = n_groups - 1
    lax.fori_loop(0, (last - 1) // 2, step_pair, 0)
    if (last - 1) % 2:
        step(last - 1, (last - 1) % 2)
    least = stage_values(last - 1, (last - 1) % 2)
    stage_weights(last % 2, False)
    stage_finish(last - 1, least)
    stage_finish(last, stage_values(last, last % 2))


def _attention(qkv):
    B, _, S, _ = qkv.shape
    rows2 = HEADS_PER_SLAB * Q_ROWS
    def slab(which):
        return pl.BlockSpec((1, 1, S, LANES), lambda bi, p: (bi, which * N_SLABS + p, 0, 0))
    return pl.pallas_call(
        _attn_kernel,
        grid=(B, N_SLABS),
        in_specs=[slab(0), slab(1), slab(2)],
        out_specs=slab(0),
        out_shape=jax.ShapeDtypeStruct((B, N_SLABS, S, LANES), BF16),
        scratch_shapes=[pltpu.VMEM((GROUP_BLOCKS, rows2, LANES), F32),
                        pltpu.VMEM((GROUP_BLOCKS, rows2, KEY_BLOCK), F32),
                        pltpu.VMEM((2, GROUP_BLOCKS, rows2, FAST_WINDOW), F32),
                        pltpu.VMEM((2, GROUP_BLOCKS, rows2, FAST_WINDOW), BF16),
                        pltpu.VMEM((2, GROUP_BLOCKS, rows2, KEY_BLOCK), F32)],
        compiler_params=_params("parallel", "parallel"),
        name="stickbreak_attn",
    )(qkv, qkv, qkv)


def _layer1_kernel(o_ref, x_ref, wo_ref, g1_ref, b1_ref, wg_ref, wu_ref, wd_ref, g2_ref, b2_ref, y_ref):
    parts = range(LAYER1_PARTS)
    os_ = [jnp.concatenate([o_ref[0, k, _part_rows(p), :] for k in range(N_SLABS)], axis=1) for p in parts]
    ms = [jnp.dot(o, wo_ref[...], preferred_element_type=F32) for o in os_]
    hs = [_layer_norm(DEEPNORM_ALPHA * x_ref[0, _part_rows(p), :] + ms[p], g1_ref[...], b1_ref[...])
          for p in parts]
    ys = _ffn_sublayers(hs, wg_ref, wu_ref, wd_ref, g2_ref[...], b2_ref[...])
    for p in parts:
        y_ref[0, _part_rows(p), :] = ys[p]


def _layer1(o, x, w_o, g1, b1, wg, wu, wd, g2, b2):
    B, S, D = x.shape
    tile = LAYER1_PARTS * PART_ROWS
    row = pl.BlockSpec((1, tile, D), lambda bi, i: (bi, i, 0))
    consts = (w_o, g1, b1, wg, wu, wd, g2, b2)
    return pl.pallas_call(
        _layer1_kernel,
        grid=(B, S // tile),
        in_specs=[pl.BlockSpec((1, N_SLABS, tile, LANES), lambda bi, i: (bi, 0, i, 0)), row]
                 + [_resident(c.shape) for c in consts],
        out_specs=row,
        out_shape=jax.ShapeDtypeStruct(x.shape, F32),
        compiler_params=_params("parallel", "parallel"),
        name="proj_ffn",
    )(o, x, *consts)


def kernel(x, ln_mix_g, ln_mix_b, ln_ffn_g, ln_ffn_b, pool_w, pool_scale, w_qkv, w_o, w_gate, w_up, w_down):
    B, S, D = x.shape
    assert D == D_MODEL and S % (LAYER0_PARTS * PART_ROWS) == 0 and S % (LAYER1_PARTS * PART_ROWS) == 0 and S % (2 * GROUP_BLOCKS * Q_ROWS) == 0
    row = lambda p, i: p[i].reshape(1, D)
    bf = lambda w: w.astype(BF16)

    h, qkv = _layer0(x, bf(pool_w[0]), row(pool_scale, 0), row(ln_mix_g, 0), row(ln_mix_b, 0),
                     bf(w_gate[0]), bf(w_up[0]), bf(w_down[0]), row(ln_ffn_g, 0), row(ln_ffn_b, 0),
                     bf(w_qkv[0]))
    o = _attention(qkv)
    return _layer1(o, h, bf(w_o[0]), row(ln_mix_g, 1), row(ln_mix_b, 1),
                   bf(w_gate[1]), bf(w_up[1]), bf(w_down[1]), row(ln_ffn_g, 1), row(ln_ffn_b, 1))
```

```python
import jax
import jax.numpy as jnp
from jax import lax
from jax.experimental import pallas as pl
from jax.experimental.pallas import tpu as pltpu

D_MODEL = 1024
DEPTH = 2
POOL_WINDOWS = (2, 4, 8, 16)
N_POOL_GROUPS = len(POOL_WINDOWS)
POOL_GROUP_W = D_MODEL // N_POOL_GROUPS
N_HEADS = 16
HEAD_DIM = D_MODEL // N_HEADS
D_FF = 2816
DEEPNORM_ALPHA = (2.0 * DEPTH) ** 0.25
LN_EPS = 1e-5

F32 = jnp.float32
BF16 = jnp.bfloat16

LANES = 128
VMEM_LIMIT_BYTES = 56 * 1024 * 1024

PART_ROWS = 512
LAYER0_PARTS = 1
LAYER1_PARTS = 2
POOL_HALO = 16
FF_CHUNK = 256
assert D_FF % FF_CHUNK == 0
N_FF_CHUNKS = D_FF // FF_CHUNK
QKV_CHUNK = 512

HEADS_PER_SLAB = LANES // HEAD_DIM
N_SLABS = N_HEADS // HEADS_PER_SLAB
Q_ROWS = 64
KEY_BLOCK = 128
WINDOW_BLOCKS = 2
FAST_WINDOW = WINDOW_BLOCKS * KEY_BLOCK
GROUP_BLOCKS = 4
EXP_UNDERFLOW = 104.0


def _layer_norm(v, g, b):
    mu = jnp.mean(v, axis=-1, keepdims=True)
    c = v - mu
    var = jnp.mean(c * c, axis=-1, keepdims=True)
    return c * lax.rsqrt(var + LN_EPS) * g + b


def _resident(shape):
    nd = len(shape)
    return pl.BlockSpec(shape, lambda *_: (0,) * nd, pipeline_mode=pl.Buffered(1))


def _resident_layer(shape, layer):
    nd = len(shape)
    return pl.BlockSpec((None,) + tuple(shape[1:]), lambda *_: (layer,) + (0,) * (nd - 1),
                        pipeline_mode=pl.Buffered(1))


def _row(ref, layer):
    return ref[pl.ds(layer, 1), :]


def _params(*sem):
    return pltpu.CompilerParams(dimension_semantics=sem, vmem_limit_bytes=VMEM_LIMIT_BYTES)


def _pool_mixer(x, halo, t0, w_ref, scale):
    rows = x.shape[0]
    t = t0 + lax.broadcasted_iota(jnp.int32, (rows, 1), 0)
    ys = []
    for gi, w in enumerate(POOL_WINDOWS):
        cols = slice(gi * POOL_GROUP_W, (gi + 1) * POOL_GROUP_W)
        xg = x[:, cols]
        s = jnp.concatenate([halo[:, cols], xg], axis=0)
        k = 1
        while k < w:
            s = s + pltpu.roll(s, k, axis=0)
            k *= 2
        cnt = jnp.minimum(t + 1, w).astype(F32)
        mix = s[POOL_HALO:, :] / cnt - xg
        ys.append(jnp.dot(mix.astype(BF16), w_ref[gi], preferred_element_type=F32))
    return jnp.concatenate(ys, axis=1) * scale


def _swiglu(x, wg_ref, wu_ref, wd_ref):
    xb = x.astype(BF16)
    acc = None
    for c in range(N_FF_CHUNKS):
        cols = slice(c * FF_CHUNK, (c + 1) * FF_CHUNK)
        gate = jnp.dot(xb, wg_ref[:, cols], preferred_element_type=F32)
        up = jnp.dot(xb, wu_ref[:, cols], preferred_element_type=F32)
        h = (gate * jax.nn.sigmoid(gate) * up).astype(BF16)
        p = jnp.dot(h, wd_ref[cols, :], preferred_element_type=F32)
        acc = p if acc is None else acc + p
    return acc


def _ffn_sublayers(hs, wg_ref, wu_ref, wd_ref, g, b):
    fs = [_swiglu(h, wg_ref, wu_ref, wd_ref) for h in hs]
    return [_layer_norm(DEEPNORM_ALPHA * h + f, g, b) for h, f in zip(hs, fs)]


def _part_rows(p):
    return slice(p * PART_ROWS, (p + 1) * PART_ROWS)


def _layer0_kernel(x_ref, halo_ref, pw_ref, ps_ref, g1_ref, b1_ref, wg_ref, wu_ref, wd_ref, g2_ref, b2_ref,
                   wqkv_ref, y_ref, qkv_ref):
    i = pl.program_id(1)
    xs = [x_ref[0, _part_rows(p), :] for p in range(LAYER0_PARTS)]
    first_halo = jnp.where(i > 0, halo_ref[0], 0.0)
    halos = [first_halo] + [x[PART_ROWS - POOL_HALO:, :] for x in xs[:-1]]
    ms = [_pool_mixer(x, halo, (i * LAYER0_PARTS + p) * PART_ROWS, pw_ref, ps_ref[...])
          for p, (x, halo) in enumerate(zip(xs, halos))]
    hs = [_layer_norm(DEEPNORM_ALPHA * x + m, _row(g1_ref, 0), _row(b1_ref, 0)) for x, m in zip(xs, ms)]
    ys = _ffn_sublayers(hs, wg_ref, wu_ref, wd_ref, _row(g2_ref, 0), _row(b2_ref, 0))
    for p, y in enumerate(ys):
        y_ref[0, _part_rows(p), :] = y
        yb = y.astype(BF16)
        for c in range(3 * D_MODEL // QKV_CHUNK):
            cols = slice(c * QKV_CHUNK, (c + 1) * QKV_CHUNK)
            out = jnp.dot(yb, wqkv_ref[:, cols], preferred_element_type=F32)
            if (c + 1) * QKV_CHUNK <= D_MODEL:
                out = out * (HEAD_DIM ** -0.5)
            out = out.astype(BF16)
            for k in range(QKV_CHUNK // LANES):
                qkv_ref[0, c * (QKV_CHUNK // LANES) + k, _part_rows(p), :] = out[:, k * LANES:(k + 1) * LANES]


def _layer0(x, pool_w, pool_scale, g1, b1, wg, wu, wd, g2, b2, w_qkv):
    B, S, D = x.shape
    tile = LAYER0_PARTS * PART_ROWS
    halo_blocks_per_tile = tile // POOL_HALO
    consts = (pool_w, pool_scale, g1, b1, wg, wu, wd, g2, b2, w_qkv)
    stacked = (wg, wu, wd)
    return pl.pallas_call(
        _layer0_kernel,
        grid=(B, S // tile),
        in_specs=[pl.BlockSpec((1, tile, D), lambda bi, i: (bi, i, 0)),
                  pl.BlockSpec((1, POOL_HALO, D),
                               lambda bi, i: (bi, jnp.maximum(i * halo_blocks_per_tile - 1, 0), 0))]
                 + [_resident_layer(c.shape, 0) if any(c is w for w in stacked) else _resident(c.shape)
                    for c in consts],
        out_specs=[pl.BlockSpec((1, tile, D), lambda bi, i: (bi, i, 0)),
                   pl.BlockSpec((1, 3 * N_SLABS, tile, LANES), lambda bi, i: (bi, 0, i, 0))],
        out_shape=[jax.ShapeDtypeStruct((B, S, D), F32),
                   jax.ShapeDtypeStruct((B, 3 * N_SLABS, S, LANES), BF16)],
        compiler_params=_params("parallel", "parallel"),
        name="layer0_qkv",
    )(x, x, *consts)


def _attn_kernel(q_ref, k_ref, v_ref, o_ref, acc_ref, carry_ref, z_ref, w_ref, tot_ref):
    seq = q_ref.shape[2]
    n_groups = seq // (Q_ROWS * GROUP_BLOCKS)
    rows2 = HEADS_PER_SLAB * Q_ROWS
    slots = range(GROUP_BLOCKS)
    lane = lax.broadcasted_iota(jnp.int32, (Q_ROWS, LANES), 1)
    first_head = lane < HEAD_DIM
    nt_dims = (((1,), (1,)), ((), ()))

    tri_r = lax.broadcasted_iota(jnp.int32, (2 * KEY_BLOCK, 2 * KEY_BLOCK), 0) % KEY_BLOCK
    tri_c = lax.broadcasted_iota(jnp.int32, (2 * KEY_BLOCK, 2 * KEY_BLOCK), 1)
    tri = jnp.where((tri_c >= KEY_BLOCK) | (tri_r >= tri_c), 1.0, 0.0).astype(BF16)
    col = lax.broadcasted_iota(jnp.int32, (rows2, KEY_BLOCK), 1)
    skew = col - lax.broadcasted_iota(jnp.int32, (rows2, KEY_BLOCK), 0) % Q_ROWS

    def query_rows(i):
        return pl.ds(pl.multiple_of(i * Q_ROWS, Q_ROWS), Q_ROWS)

    def stacked_queries(i):
        q = q_ref[0, 0, query_rows(i), :]
        zero = jnp.zeros_like(q)
        return jnp.concatenate([jnp.where(first_head, q, zero), jnp.where(first_head, zero, q)], axis=0)

    def softplus(z):
        return jnp.maximum(z, 0.0) + jnp.log(1.0 + jnp.exp(-jnp.abs(z)))

    def hi_lo(sp):
        hi = sp.astype(BF16)
        lo = (sp - hi.astype(F32)).astype(BF16)
        return jnp.concatenate([hi, lo], axis=1)


    def window_start(i):
        return (i + 1) * Q_ROWS - FAST_WINDOW

    def window_rows(i):
        return pl.ds(pl.multiple_of(jnp.maximum(window_start(i), 0), Q_ROWS), FAST_WINDOW)

    def store_block(i, acc):
        o_ref[0, 0, query_rows(i), :] = jnp.where(first_head, acc[:Q_ROWS], acc[Q_ROWS:]).astype(BF16)

    def stage_scores(g, buf):
        for slot in slots:
            i = g * GROUP_BLOCKS + slot
            z_ref[buf, slot] = lax.dot_general(stacked_queries(i), k_ref[0, 0, window_rows(i), :],
                                               nt_dims, preferred_element_type=F32)

    def stage_weights(buf, first_group):
        def visible(slot, b):
            i = slot if first_group else GROUP_BLOCKS
            bound = i * Q_ROWS - max(window_start(i), 0) - b * KEY_BLOCK
            return None if bound >= KEY_BLOCK else skew < bound

        zs, masks, parts = [], [], []
        for slot in slots:
            z = z_ref[buf, slot]
            for b in range(WINDOW_BLOCKS):
                zb = z[:, b * KEY_BLOCK:(b + 1) * KEY_BLOCK]
                m = visible(slot, b)
                sp = softplus(zb)
                zs.append(zb)
                masks.append(m)
                parts.append(hi_lo(sp if m is None else jnp.where(m, sp, 0.0)))
        st = jnp.dot(jnp.concatenate(parts, axis=0), tri, preferred_element_type=F32)
        for slot in slots:
            carry = None
            a = [None] * WINDOW_BLOCKS
            for b in reversed(range(WINDOW_BLOCKS)):
                n = slot * WINDOW_BLOCKS + b
                sb = st[n * rows2:(n + 1) * rows2]
                x = zs[n] - sb[:, :KEY_BLOCK]
                if carry is not None:
                    x = x - carry
                e = jnp.exp(x)
                a[b] = (e if masks[n] is None else jnp.where(masks[n], e, 0.0)).astype(BF16)
                carry = sb[:, KEY_BLOCK:] if carry is None else carry + sb[:, KEY_BLOCK:]
            w_ref[buf, slot] = jnp.concatenate(a, axis=1)
            tot_ref[buf, slot] = carry

    def stage_values(g, buf, least):
        for slot in slots:
            i = g * GROUP_BLOCKS + slot
            acc = jnp.dot(w_ref[buf, slot], v_ref[0, 0, window_rows(i), :], preferred_element_type=F32)
            total = tot_ref[buf, slot]
            acc_ref[i] = acc
            carry_ref[i] = total
            store_block(i, acc)
            least = jnp.minimum(least, jnp.min(total))
        return least

    def step(g, buf, least):
        least = stage_values(g - 1, 1 - buf, least)
        stage_scores(g + 1, 1 - buf)
        stage_weights(buf, False)
        return least

    def step_pair(p, least):
        return step(2 + 2 * p, 0, step(1 + 2 * p, 1, least))

    assert n_groups >= 2 and n_groups * GROUP_BLOCKS * Q_ROWS == seq
    assert window_start(GROUP_BLOCKS) >= 0
    stage_scores(0, 0)
    stage_scores(1, 1)
    stage_weights(0, True)
    last = n_groups - 1
    least = lax.fori_loop(0, (last - 1) // 2, step_pair, jnp.float32(jnp.inf))
    if (last - 1) % 2:
        least = step(last - 1, (last - 1) % 2, least)
    least = stage_values(last - 1, (last - 1) % 2, least)
    stage_weights(last % 2, False)
    least = stage_values(last, last % 2, least)


    def sweep(i, start_raw, limit):
        start = pl.multiple_of(jnp.maximum(start_raw, 0), Q_ROWS)
        rows = pl.ds(start, KEY_BLOCK)
        z = lax.dot_general(stacked_queries(i), k_ref[0, 0, rows, :], nt_dims, preferred_element_type=F32)
        valid = jnp.logical_and(skew < i * Q_ROWS - start, col < limit - start)
        sp = jnp.where(valid, softplus(z), 0.0)
        st = jnp.dot(hi_lo(sp), tri, preferred_element_type=F32)
        carry = carry_ref[i]
        a = jnp.where(valid, jnp.exp(z - st[:, :KEY_BLOCK] - carry), 0.0)
        acc_ref[i] += jnp.dot(a.astype(BF16), v_ref[0, 0, rows, :], preferred_element_type=F32)
        carry = carry + st[:, KEY_BLOCK:]
        carry_ref[i] = carry
        return jnp.min(carry)

    def finish_block(i, _):
        def more(state):
            start_raw, _, least = state
            return jnp.logical_and(start_raw > -KEY_BLOCK, least <= EXP_UNDERFLOW)

        def back(state):
            start_raw, limit, _ = state
            return start_raw - KEY_BLOCK, jnp.maximum(start_raw, 0), sweep(i, start_raw, limit)

        lax.while_loop(more, back, (window_start(i) - KEY_BLOCK, window_start(i), jnp.min(carry_ref[i])))
        store_block(i, acc_ref[i])
        return 0

    @pl.when(least <= EXP_UNDERFLOW)
    def _():
        lax.fori_loop(0, n_groups * GROUP_BLOCKS, finish_block, 0)


def _attention(qkv):
    B, _, S, _ = qkv.shape
    rows2 = HEADS_PER_SLAB * Q_ROWS
    def slab(which):
        return pl.BlockSpec((1, 1, S, LANES), lambda bi, p: (bi, which * N_SLABS + p, 0, 0))
    return pl.pallas_call(
        _attn_kernel,
        grid=(B, N_SLABS),
        in_specs=[slab(0), slab(1), slab(2)],
        out_specs=slab(0),
        out_shape=jax.ShapeDtypeStruct((B, N_SLABS, S, LANES), BF16),
        scratch_shapes=[pltpu.VMEM((S // Q_ROWS, rows2, LANES), F32),
                        pltpu.VMEM((S // Q_ROWS, rows2, KEY_BLOCK), F32),
                        pltpu.VMEM((2, GROUP_BLOCKS, rows2, FAST_WINDOW), F32),
                        pltpu.VMEM((2, GROUP_BLOCKS, rows2, FAST_WINDOW), BF16),
                        pltpu.VMEM((2, GROUP_BLOCKS, rows2, KEY_BLOCK), F32)],
        compiler_params=_params("parallel", "parallel"),
        name="stickbreak_attn",
    )(qkv, qkv, qkv)


def _layer1_kernel(o_ref, x_ref, wo_ref, g1_ref, b1_ref, wg_ref, wu_ref, wd_ref, g2_ref, b2_ref, y_ref):
    parts = range(LAYER1_PARTS)
    os_ = [jnp.concatenate([o_ref[0, k, _part_rows(p), :] for k in range(N_SLABS)], axis=1) for p in parts]
    ms = [jnp.dot(o, wo_ref[...], preferred_element_type=F32) for o in os_]
    hs = [_layer_norm(DEEPNORM_ALPHA * x_ref[0, _part_rows(p), :] + ms[p], _row(g1_ref, 1), _row(b1_ref, 1))
          for p in parts]
    ys = _ffn_sublayers(hs, wg_ref, wu_ref, wd_ref, _row(g2_ref, 1), _row(b2_ref, 1))
    for p in parts:
        y_ref[0, _part_rows(p), :] = ys[p]


def _layer1(o, x, w_o, g1, b1, wg, wu, wd, g2, b2):
    B, S, D = x.shape
    tile = LAYER1_PARTS * PART_ROWS
    row = pl.BlockSpec((1, tile, D), lambda bi, i: (bi, i, 0))
    consts = (w_o, g1, b1, wg, wu, wd, g2, b2)
    stacked = (wg, wu, wd)
    return pl.pallas_call(
        _layer1_kernel,
        grid=(B, S // tile),
        in_specs=[pl.BlockSpec((1, N_SLABS, tile, LANES), lambda bi, i: (bi, 0, i, 0)), row]
                 + [_resident_layer(c.shape, 1) if any(c is w for w in stacked) else _resident(c.shape)
                    for c in consts],
        out_specs=row,
        out_shape=jax.ShapeDtypeStruct(x.shape, F32),
        compiler_params=_params("parallel", "parallel"),
        name="proj_ffn",
    )(o, x, *consts)


def kernel(x, ln_mix_g, ln_mix_b, ln_ffn_g, ln_ffn_b, pool_w, pool_scale, w_qkv, w_o, w_gate, w_up, w_down):
    B, S, D = x.shape
    assert D == D_MODEL and S % (LAYER0_PARTS * PART_ROWS) == 0 and S % (LAYER1_PARTS * PART_ROWS) == 0
    assert S % (2 * GROUP_BLOCKS * Q_ROWS) == 0
    assert w_gate.shape[0] == DEPTH and pool_w.shape[0] == 1 and w_qkv.shape[0] == 1 and w_o.shape[0] == 1
    wg, wu, wd = w_gate.astype(BF16), w_up.astype(BF16), w_down.astype(BF16)
    h, qkv = _layer0(x, pool_w[0].astype(BF16), pool_scale, ln_mix_g, ln_mix_b, wg, wu, wd,
                     ln_ffn_g, ln_ffn_b, w_qkv[0].astype(BF16))
    o = _attention(qkv)
    return _layer1(o, h, w_o[0].astype(BF16), ln_mix_g, ln_mix_b, wg, wu, wd, ln_ffn_g, ln_ffn_b)
```

```python
import jax
import jax.numpy as jnp
from jax import lax
from jax.experimental import pallas as pl
from jax.experimental.pallas import tpu as pltpu

D_MODEL = 1024
DEPTH = 2
POOL_WINDOWS = (2, 4, 8, 16)
N_POOL_GROUPS = len(POOL_WINDOWS)
POOL_GROUP_W = D_MODEL // N_POOL_GROUPS
N_HEADS = 16
HEAD_DIM = D_MODEL // N_HEADS
D_FF = 2816
DEEPNORM_ALPHA = (2.0 * DEPTH) ** 0.25
LN_EPS = 1e-5

F32 = jnp.float32
BF16 = jnp.bfloat16

LANES = 128
VMEM_LIMIT_BYTES = 56 * 1024 * 1024

PART_ROWS = 512
LAYER0_PARTS = 1
LAYER1_PARTS = 2
POOL_HALO = 16
FF_CHUNK = 256
assert D_FF % FF_CHUNK == 0
N_FF_CHUNKS = D_FF // FF_CHUNK
QKV_CHUNK = 512

HEADS_PER_SLAB = LANES // HEAD_DIM
N_SLABS = N_HEADS // HEADS_PER_SLAB
Q_ROWS = 64
KEY_BLOCK = 128
WINDOW_BLOCKS = 2
FAST_WINDOW = WINDOW_BLOCKS * KEY_BLOCK
GROUP_BLOCKS = 4
EXP_UNDERFLOW = 104.0


def _layer_norm(v, g, b):
    mu = jnp.mean(v, axis=-1, keepdims=True)
    c = v - mu
    var = jnp.mean(c * c, axis=-1, keepdims=True)
    return c * lax.rsqrt(var + LN_EPS) * g + b


def _resident(shape):
    nd = len(shape)
    return pl.BlockSpec(shape, lambda *_: (0,) * nd, pipeline_mode=pl.Buffered(1))


def _resident_layer(shape, layer):
    nd = len(shape)
    return pl.BlockSpec((None,) + tuple(shape[1:]), lambda *_: (layer,) + (0,) * (nd - 1),
                        pipeline_mode=pl.Buffered(1))


def _row(ref, layer):
    return ref[pl.ds(layer, 1), :]


def _params(*sem):
    return pltpu.CompilerParams(dimension_semantics=sem, vmem_limit_bytes=VMEM_LIMIT_BYTES)


def _pool_mixer(x, halo, t0, w_ref, scale):
    rows = x.shape[0]
    t = t0 + lax.broadcasted_iota(jnp.int32, (rows, 1), 0)
    ys = []
    for gi, w in enumerate(POOL_WINDOWS):
        cols = slice(gi * POOL_GROUP_W, (gi + 1) * POOL_GROUP_W)
        xg = x[:, cols]
        s = jnp.concatenate([halo[:, cols], xg], axis=0)
        k = 1
        while k < w:
            s = s + pltpu.roll(s, k, axis=0)
            k *= 2
        cnt = jnp.minimum(t + 1, w).astype(F32)
        mix = s[POOL_HALO:, :] / cnt - xg
        ys.append(jnp.dot(mix.astype(BF16), w_ref[gi], preferred_element_type=F32))
    return jnp.concatenate(ys, axis=1) * scale


def _swiglu(x, wg_ref, wu_ref, wd_ref):
    xb = x.astype(BF16)
    acc = None
    for c in range(N_FF_CHUNKS):
        cols = slice(c * FF_CHUNK, (c + 1) * FF_CHUNK)
        gate = jnp.dot(xb, wg_ref[:, cols], preferred_element_type=F32)
        up = jnp.dot(xb, wu_ref[:, cols], preferred_element_type=F32)
        h = (gate * jax.nn.sigmoid(gate) * up).astype(BF16)
        p = jnp.dot(h, wd_ref[cols, :], preferred_element_type=F32)
        acc = p if acc is None else acc + p
    return acc


def _ffn_sublayers(hs, wg_ref, wu_ref, wd_ref, g, b):
    fs = [_swiglu(h, wg_ref, wu_ref, wd_ref) for h in hs]
    return [_layer_norm(DEEPNORM_ALPHA * h + f, g, b) for h, f in zip(hs, fs)]


def _part_rows(p):
    return slice(p * PART_ROWS, (p + 1) * PART_ROWS)


def _layer0_kernel(x_ref, halo_ref, pw_ref, ps_ref, g1_ref, b1_ref, wg_ref, wu_ref, wd_ref, g2_ref, b2_ref,
                   wqkv_ref, y_ref, qkv_ref):
    i = pl.program_id(1)
    xs = [x_ref[0, _part_rows(p), :] for p in range(LAYER0_PARTS)]
    first_halo = jnp.where(i > 0, halo_ref[0], 0.0)
    halos = [first_halo] + [x[PART_ROWS - POOL_HALO:, :] for x in xs[:-1]]
    ms = [_pool_mixer(x, halo, (i * LAYER0_PARTS + p) * PART_ROWS, pw_ref, ps_ref[...])
          for p, (x, halo) in enumerate(zip(xs, halos))]
    hs = [_layer_norm(DEEPNORM_ALPHA * x + m, _row(g1_ref, 0), _row(b1_ref, 0)) for x, m in zip(xs, ms)]
    ys = _ffn_sublayers(hs, wg_ref, wu_ref, wd_ref, _row(g2_ref, 0), _row(b2_ref, 0))
    for p, y in enumerate(ys):
        y_ref[0, _part_rows(p), :] = y
        yb = y.astype(BF16)
        for c in range(3 * D_MODEL // QKV_CHUNK):
            cols = slice(c * QKV_CHUNK, (c + 1) * QKV_CHUNK)
            out = jnp.dot(yb, wqkv_ref[:, cols], preferred_element_type=F32)
            if (c + 1) * QKV_CHUNK <= D_MODEL:
                out = out * (HEAD_DIM ** -0.5)
            out = out.astype(BF16)
            for k in range(QKV_CHUNK // LANES):
                qkv_ref[0, c * (QKV_CHUNK // LANES) + k, _part_rows(p), :] = out[:, k * LANES:(k + 1) * LANES]


def _layer0(x, pool_w, pool_scale, g1, b1, wg, wu, wd, g2, b2, w_qkv):
    B, S, D = x.shape
    tile = LAYER0_PARTS * PART_ROWS
    halo_blocks_per_tile = tile // POOL_HALO
    consts = (pool_w, pool_scale, g1, b1, wg, wu, wd, g2, b2, w_qkv)
    stacked = (wg, wu, wd)
    return pl.pallas_call(
        _layer0_kernel,
        grid=(B, S // tile),
        in_specs=[pl.BlockSpec((1, tile, D), lambda bi, i: (bi, i, 0)),
                  pl.BlockSpec((1, POOL_HALO, D),
                               lambda bi, i: (bi, jnp.maximum(i * halo_blocks_per_tile - 1, 0), 0))]
                 + [_resident_layer(c.shape, 0) if any(c is w for w in stacked) else _resident(c.shape)
                    for c in consts],
        out_specs=[pl.BlockSpec((1, tile, D), lambda bi, i: (bi, i, 0)),
                   pl.BlockSpec((1, 3 * N_SLABS, tile, LANES), lambda bi, i: (bi, 0, i, 0))],
        out_shape=[jax.ShapeDtypeStruct((B, S, D), F32),
                   jax.ShapeDtypeStruct((B, 3 * N_SLABS, S, LANES), BF16)],
        compiler_params=_params("parallel", "parallel"),
        name="layer0_qkv",
    )(x, x, *consts)


def _attn_kernel(q_ref, k_ref, v_ref, o_ref, acc_ref, carry_ref, z_ref, w_ref, tot_ref):
    seq = q_ref.shape[2]
    n_groups = seq // (Q_ROWS * GROUP_BLOCKS)
    rows2 = HEADS_PER_SLAB * Q_ROWS
    slots = range(GROUP_BLOCKS)
    lane = lax.broadcasted_iota(jnp.int32, (Q_ROWS, LANES), 1)
    first_head = lane < HEAD_DIM
    nt_dims = (((1,), (1,)), ((), ()))

    tri_r = lax.broadcasted_iota(jnp.int32, (2 * KEY_BLOCK, 2 * KEY_BLOCK), 0) % KEY_BLOCK
    tri_c = lax.broadcasted_iota(jnp.int32, (2 * KEY_BLOCK, 2 * KEY_BLOCK), 1)
    tri = jnp.where((tri_c >= KEY_BLOCK) | (tri_r >= tri_c), 1.0, 0.0).astype(BF16)
    col = lax.broadcasted_iota(jnp.int32, (rows2, KEY_BLOCK), 1)
    skew = col - lax.broadcasted_iota(jnp.int32, (rows2, KEY_BLOCK), 0) % Q_ROWS

    def query_rows(i):
        return pl.ds(pl.multiple_of(i * Q_ROWS, Q_ROWS), Q_ROWS)

    def stacked_queries(i):
        q = q_ref[0, 0, query_rows(i), :]
        zero = jnp.zeros_like(q)
        return jnp.concatenate([jnp.where(first_head, q, zero), jnp.where(first_head, zero, q)], axis=0)

    def softplus(z):
        return jnp.maximum(z, 0.0) + jnp.log(1.0 + jnp.exp(-jnp.abs(z)))

    def hi_lo(sp):
        hi = sp.astype(BF16)
        lo = (sp - hi.astype(F32)).astype(BF16)
        return jnp.concatenate([hi, lo], axis=1)


    def window_start(i):
        return (i + 1) * Q_ROWS - FAST_WINDOW

    def window_rows(i):
        return pl.ds(pl.multiple_of(jnp.maximum(window_start(i), 0), Q_ROWS), FAST_WINDOW)

    def store_block(i, acc):
        o_ref[0, 0, query_rows(i), :] = jnp.where(first_head, acc[:Q_ROWS], acc[Q_ROWS:]).astype(BF16)

    def stage_scores(g, buf):
        for slot in slots:
            i = g * GROUP_BLOCKS + slot
            z_ref[buf, slot] = lax.dot_general(stacked_queries(i), k_ref[0, 0, window_rows(i), :],
                                               nt_dims, preferred_element_type=F32)

    def stage_weights(buf, first_group):
        def visible(slot, b):
            i = slot if first_group else GROUP_BLOCKS
            bound = i * Q_ROWS - max(window_start(i), 0) - b * KEY_BLOCK
            return None if bound >= KEY_BLOCK else skew < bound

        zs, masks, parts = [], [], []
        for slot in slots:
            z = z_ref[buf, slot]
            for b in range(WINDOW_BLOCKS):
                zb = z[:, b * KEY_BLOCK:(b + 1) * KEY_BLOCK]
                m = visible(slot, b)
                sp = softplus(zb)
                zs.append(zb)
                masks.append(m)
                parts.append(hi_lo(sp if m is None else jnp.where(m, sp, 0.0)))
        st = jnp.dot(jnp.concatenate(parts, axis=0), tri, preferred_element_type=F32)
        for slot in slots:
            carry = None
            a = [None] * WINDOW_BLOCKS
            for b in reversed(range(WINDOW_BLOCKS)):
                n = slot * WINDOW_BLOCKS + b
                sb = st[n * rows2:(n + 1) * rows2]
                x = zs[n] - sb[:, :KEY_BLOCK]
                if carry is not None:
                    x = x - carry
                e = jnp.exp(x)
                a[b] = (e if masks[n] is None else jnp.where(masks[n], e, 0.0)).astype(BF16)
                carry = sb[:, KEY_BLOCK:] if carry is None else carry + sb[:, KEY_BLOCK:]
            w_ref[buf, slot] = jnp.concatenate(a, axis=1)
            tot_ref[buf, slot] = carry

    def stage_values(g, buf, least):
        for slot in slots:
            i = g * GROUP_BLOCKS + slot
            acc = jnp.dot(w_ref[buf, slot], v_ref[0, 0, window_rows(i), :], preferred_element_type=F32)
            total = tot_ref[buf, slot]
            acc_ref[i] = acc
            carry_ref[i] = total
            store_block(i, acc)
            keys_left = window_start(i) > 0
            least = jnp.minimum(least, jnp.where(keys_left, jnp.min(total), jnp.inf))
        return least

    def step(g, buf, least):
        least = stage_values(g - 1, 1 - buf, least)
        stage_scores(g + 1, 1 - buf)
        stage_weights(buf, False)
        return least

    def step_pair(p, least):
        return step(2 + 2 * p, 0, step(1 + 2 * p, 1, least))

    assert n_groups >= 2 and n_groups * GROUP_BLOCKS * Q_ROWS == seq
    assert window_start(GROUP_BLOCKS) >= 0
    stage_scores(0, 0)
    stage_scores(1, 1)
    stage_weights(0, True)
    last = n_groups - 1
    least = lax.fori_loop(0, (last - 1) // 2, step_pair, jnp.float32(jnp.inf))
    if (last - 1) % 2:
        least = step(last - 1, (last - 1) % 2, least)
    least = stage_values(last - 1, (last - 1) % 2, least)
    stage_weights(last % 2, False)
    least = stage_values(last, last % 2, least)


    def sweep(i, start_raw, limit):
        start = pl.multiple_of(jnp.maximum(start_raw, 0), Q_ROWS)
        rows = pl.ds(start, KEY_BLOCK)
        z = lax.dot_general(stacked_queries(i), k_ref[0, 0, rows, :], nt_dims, preferred_element_type=F32)
        valid = jnp.logical_and(skew < i * Q_ROWS - start, col < limit - start)
        sp = jnp.where(valid, softplus(z), 0.0)
        st = jnp.dot(hi_lo(sp), tri, preferred_element_type=F32)
        carry = carry_ref[i]
        a = jnp.where(valid, jnp.exp(z - st[:, :KEY_BLOCK] - carry), 0.0)
        acc_ref[i] += jnp.dot(a.astype(BF16), v_ref[0, 0, rows, :], preferred_element_type=F32)
        carry = carry + st[:, KEY_BLOCK:]
        carry_ref[i] = carry
        return jnp.min(carry)

    def finish_block(i, _):
        def more(state):
            start_raw, _, least = state
            return jnp.logical_and(start_raw > -KEY_BLOCK, least <= EXP_UNDERFLOW)

        def back(state):
            start_raw, limit, _ = state
            return start_raw - KEY_BLOCK, jnp.maximum(start_raw, 0), sweep(i, start_raw, limit)

        lax.while_loop(more, back, (window_start(i) - KEY_BLOCK, window_start(i), jnp.min(carry_ref[i])))
        store_block(i, acc_ref[i])
        return 0

    @pl.when(least <= EXP_UNDERFLOW)
    def _():
        lax.fori_loop(0, n_groups * GROUP_BLOCKS, finish_block, 0)


def _attention(qkv):
    B, _, S, _ = qkv.shape
    rows2 = HEADS_PER_SLAB * Q_ROWS
    def slab(which):
        return pl.BlockSpec((1, 1, S, LANES), lambda bi, p: (bi, which * N_SLABS + p, 0, 0))
    return pl.pallas_call(
        _attn_kernel,
        grid=(B, N_SLABS),
        in_specs=[slab(0), slab(1), slab(2)],
        out_specs=slab(0),
        out_shape=jax.ShapeDtypeStruct((B, N_SLABS, S, LANES), BF16),
        scratch_shapes=[pltpu.VMEM((S // Q_ROWS, rows2, LANES), F32),
                        pltpu.VMEM((S // Q_ROWS, rows2, KEY_BLOCK), F32),
                        pltpu.VMEM((2, GROUP_BLOCKS, rows2, FAST_WINDOW), F32),
                        pltpu.VMEM((2, GROUP_BLOCKS, rows2, FAST_WINDOW), BF16),
                        pltpu.VMEM((2, GROUP_BLOCKS, rows2, KEY_BLOCK), F32)],
        compiler_params=_params("parallel", "parallel"),
        name="stickbreak_attn",
    )(qkv, qkv, qkv)


def _layer1_kernel(o_ref, x_ref, wo_ref, g1_ref, b1_ref, wg_ref, wu_ref, wd_ref, g2_ref, b2_ref, y_ref):
    parts = range(LAYER1_PARTS)
    os_ = [jnp.concatenate([o_ref[0, k, _part_rows(p), :] for k in range(N_SLABS)], axis=1) for p in parts]
    ms = [jnp.dot(o, wo_ref[...], preferred_element_type=F32) for o in os_]
    hs = [_layer_norm(DEEPNORM_ALPHA * x_ref[0, _part_rows(p), :] + ms[p], _row(g1_ref, 1), _row(b1_ref, 1))
          for p in parts]
    ys = _ffn_sublayers(hs, wg_ref, wu_ref, wd_ref, _row(g2_ref, 1), _row(b2_ref, 1))
    for p in parts:
        y_ref[0, _part_rows(p), :] = ys[p]


def _layer1(o, x, w_o, g1, b1, wg, wu, wd, g2, b2):
    B, S, D = x.shape
    tile = LAYER1_PARTS * PART_ROWS
    row = pl.BlockSpec((1, tile, D), lambda bi, i: (bi, i, 0))
    consts = (w_o, g1, b1, wg, wu, wd, g2, b2)
    stacked = (wg, wu, wd)
    return pl.pallas_call(
        _layer1_kernel,
        grid=(B, S // tile),
        in_specs=[pl.BlockSpec((1, N_SLABS, tile, LANES), lambda bi, i: (bi, 0, i, 0)), row]
                 + [_resident_layer(c.shape, 1) if any(c is w for w in stacked) else _resident(c.shape)
                    for c in consts],
        out_specs=row,
        out_shape=jax.ShapeDtypeStruct(x.shape, F32),
        compiler_params=_params("parallel", "parallel"),
        name="proj_ffn",
    )(o, x, *consts)


def kernel(x, ln_mix_g, ln_mix_b, ln_ffn_g, ln_ffn_b, pool_w, pool_scale, w_qkv, w_o, w_gate, w_up, w_down):
    B, S, D = x.shape
    assert D == D_MODEL and S % (LAYER0_PARTS * PART_ROWS) == 0 and S % (LAYER1_PARTS * PART_ROWS) == 0
    assert S % (2 * GROUP_BLOCKS * Q_ROWS) == 0
    assert w_gate.shape[0] == DEPTH and pool_w.shape[0] == 1 and w_qkv.shape[0] == 1 and w_o.shape[0] == 1
    wg, wu, wd = w_gate.astype(BF16), w_up.astype(BF16), w_down.astype(BF16)
    h, qkv = _layer0(x, pool_w[0].astype(BF16), pool_scale, ln_mix_g, ln_mix_b, wg, wu, wd,
                     ln_ffn_g, ln_ffn_b, w_qkv[0].astype(BF16))
    o = _attention(qkv)
    return _layer1(o, h, w_o[0].astype(BF16), ln_mix_g, ln_mix_b, wg, wu, wd, ln_ffn_g, ln_ffn_b)
```

```python
import jax
import jax.numpy as jnp
from jax import lax
from jax.experimental import pallas as pl
from jax.experimental.pallas import tpu as pltpu

D_MODEL = 1024
DEPTH = 2
POOL_WINDOWS = (2, 4, 8, 16)
N_POOL_GROUPS = len(POOL_WINDOWS)
POOL_GROUP_W = D_MODEL // N_POOL_GROUPS
N_HEADS = 16
HEAD_DIM = D_MODEL // N_HEADS
D_FF = 2816
DEEPNORM_ALPHA = (2.0 * DEPTH) ** 0.25
LN_EPS = 1e-5

F32 = jnp.float32
BF16 = jnp.bfloat16

LANES = 128
VMEM_LIMIT_BYTES = 56 * 1024 * 1024

PART_ROWS = 512
LAYER0_PARTS = 1
LAYER1_PARTS = 2
POOL_HALO = 16
FF_CHUNK = 256
assert D_FF % FF_CHUNK == 0
N_FF_CHUNKS = D_FF // FF_CHUNK
QKV_CHUNK = 512

HEADS_PER_SLAB = LANES // HEAD_DIM
N_SLABS = N_HEADS // HEADS_PER_SLAB
Q_ROWS = 64
KEY_BLOCK = 128
WINDOW_BLOCKS = 2
FAST_WINDOW = WINDOW_BLOCKS * KEY_BLOCK
GROUP_BLOCKS = 4
EXP_UNDERFLOW = 104.0
SOFTPLUS_CAP = 80.0
HIDDEN = -1e30


def _layer_norm(v, g, b):
    mu = jnp.mean(v, axis=-1, keepdims=True)
    c = v - mu
    var = jnp.mean(c * c, axis=-1, keepdims=True)
    return c * lax.rsqrt(var + LN_EPS) * g + b


def _resident(shape):
    nd = len(shape)
    return pl.BlockSpec(shape, lambda *_: (0,) * nd, pipeline_mode=pl.Buffered(1))


def _resident_layer(shape, layer):
    nd = len(shape)
    return pl.BlockSpec((None,) + tuple(shape[1:]), lambda *_: (layer,) + (0,) * (nd - 1),
                        pipeline_mode=pl.Buffered(1))


def _row(ref, layer):
    return ref[pl.ds(layer, 1), :]


def _params(*sem):
    return pltpu.CompilerParams(dimension_semantics=sem, vmem_limit_bytes=VMEM_LIMIT_BYTES)


def _pool_mixer(x, halo, t0, w_ref, scale):
    rows = x.shape[0]
    t = t0 + lax.broadcasted_iota(jnp.int32, (rows, 1), 0)
    ys = []
    for gi, w in enumerate(POOL_WINDOWS):
        cols = slice(gi * POOL_GROUP_W, (gi + 1) * POOL_GROUP_W)
        xg = x[:, cols]
        s = jnp.concatenate([halo[:, cols], xg], axis=0)
        k = 1
        while k < w:
            s = s + pltpu.roll(s, k, axis=0)
            k *= 2
        cnt = jnp.minimum(t + 1, w).astype(F32)
        mix = s[POOL_HALO:, :] / cnt - xg
        ys.append(jnp.dot(mix.astype(BF16), w_ref[gi], preferred_element_type=F32))
    return jnp.concatenate(ys, axis=1) * scale


def _swiglu(x, wg_ref, wu_ref, wd_ref):
    xb = x.astype(BF16)
    acc = None
    for c in range(N_FF_CHUNKS):
        cols = slice(c * FF_CHUNK, (c + 1) * FF_CHUNK)
        gate = jnp.dot(xb, wg_ref[:, cols], preferred_element_type=F32)
        up = jnp.dot(xb, wu_ref[:, cols], preferred_element_type=F32)
        h = (gate * jax.nn.sigmoid(gate) * up).astype(BF16)
        p = jnp.dot(h, wd_ref[cols, :], preferred_element_type=F32)
        acc = p if acc is None else acc + p
    return acc


def _ffn_sublayers(hs, wg_ref, wu_ref, wd_ref, g, b):
    fs = [_swiglu(h, wg_ref, wu_ref, wd_ref) for h in hs]
    return [_layer_norm(DEEPNORM_ALPHA * h + f, g, b) for h, f in zip(hs, fs)]


def _part_rows(p):
    return slice(p * PART_ROWS, (p + 1) * PART_ROWS)


def _layer0_kernel(x_ref, halo_ref, pw_ref, ps_ref, g1_ref, b1_ref, wg_ref, wu_ref, wd_ref, g2_ref, b2_ref,
                   wqkv_ref, y_ref, qkv_ref):
    i = pl.program_id(1)
    xs = [x_ref[0, _part_rows(p), :] for p in range(LAYER0_PARTS)]
    first_halo = jnp.where(i > 0, halo_ref[0], 0.0)
    halos = [first_halo] + [x[PART_ROWS - POOL_HALO:, :] for x in xs[:-1]]
    ms = [_pool_mixer(x, halo, (i * LAYER0_PARTS + p) * PART_ROWS, pw_ref, ps_ref[...])
          for p, (x, halo) in enumerate(zip(xs, halos))]
    hs = [_layer_norm(DEEPNORM_ALPHA * x + m, _row(g1_ref, 0), _row(b1_ref, 0)) for x, m in zip(xs, ms)]
    ys = _ffn_sublayers(hs, wg_ref, wu_ref, wd_ref, _row(g2_ref, 0), _row(b2_ref, 0))
    for p, y in enumerate(ys):
        y_ref[0, _part_rows(p), :] = y
        yb = y.astype(BF16)
        for c in range(3 * D_MODEL // QKV_CHUNK):
            cols = slice(c * QKV_CHUNK, (c + 1) * QKV_CHUNK)
            out = jnp.dot(yb, wqkv_ref[:, cols], preferred_element_type=F32)
            if (c + 1) * QKV_CHUNK <= D_MODEL:
                out = out * (HEAD_DIM ** -0.5)
            out = out.astype(BF16)
            for k in range(QKV_CHUNK // LANES):
                qkv_ref[0, c * (QKV_CHUNK // LANES) + k, _part_rows(p), :] = out[:, k * LANES:(k + 1) * LANES]


def _layer0(x, pool_w, pool_scale, g1, b1, wg, wu, wd, g2, b2, w_qkv):
    B, S, D = x.shape
    tile = LAYER0_PARTS * PART_ROWS
    halo_blocks_per_tile = tile // POOL_HALO
    consts = (pool_w, pool_scale, g1, b1, wg, wu, wd, g2, b2, w_qkv)
    stacked = (wg, wu, wd)
    return pl.pallas_call(
        _layer0_kernel,
        grid=(B, S // tile),
        in_specs=[pl.BlockSpec((1, tile, D), lambda bi, i: (bi, i, 0)),
                  pl.BlockSpec((1, POOL_HALO, D),
                               lambda bi, i: (bi, jnp.maximum(i * halo_blocks_per_tile - 1, 0), 0))]
                 + [_resident_layer(c.shape, 0) if any(c is w for w in stacked) else _resident(c.shape)
                    for c in consts],
        out_specs=[pl.BlockSpec((1, tile, D), lambda bi, i: (bi, i, 0)),
                   pl.BlockSpec((1, 3 * N_SLABS, tile, LANES), lambda bi, i: (bi, 0, i, 0))],
        out_shape=[jax.ShapeDtypeStruct((B, S, D), F32),
                   jax.ShapeDtypeStruct((B, 3 * N_SLABS, S, LANES), BF16)],
        compiler_params=_params("parallel", "parallel"),
        name="layer0_qkv",
    )(x, x, *consts)


def _attn_kernel(q_ref, k_ref, v_ref, o_ref, acc_ref, carry_ref, z_ref, w_ref, tot_ref):
    seq = q_ref.shape[2]
    n_groups = seq // (Q_ROWS * GROUP_BLOCKS)
    rows2 = HEADS_PER_SLAB * Q_ROWS
    slots = range(GROUP_BLOCKS)
    lane = lax.broadcasted_iota(jnp.int32, (Q_ROWS, LANES), 1)
    first_head = lane < HEAD_DIM
    nt_dims = (((1,), (1,)), ((), ()))

    tri_r = lax.broadcasted_iota(jnp.int32, (2 * KEY_BLOCK, 2 * KEY_BLOCK), 0) % KEY_BLOCK
    tri_c = lax.broadcasted_iota(jnp.int32, (2 * KEY_BLOCK, 2 * KEY_BLOCK), 1)
    tri = jnp.where((tri_c >= KEY_BLOCK) | (tri_r >= tri_c), 1.0, 0.0).astype(BF16)
    col = lax.broadcasted_iota(jnp.int32, (rows2, KEY_BLOCK), 1)
    skew = col - lax.broadcasted_iota(jnp.int32, (rows2, KEY_BLOCK), 0) % Q_ROWS

    def query_rows(i):
        return pl.ds(pl.multiple_of(i * Q_ROWS, Q_ROWS), Q_ROWS)

    def stacked_queries(i):
        q = q_ref[0, 0, query_rows(i), :]
        zero = jnp.zeros_like(q)
        return jnp.concatenate([jnp.where(first_head, q, zero), jnp.where(first_head, zero, q)], axis=0)

    def softplus(z):
        return jnp.maximum(z, jnp.log(1.0 + jnp.exp(jnp.minimum(z, SOFTPLUS_CAP))))

    def hide(z, mask):
        return z if mask is None else jnp.where(mask, z, HIDDEN)

    def hi_lo(sp):
        hi = sp.astype(BF16)
        lo = (sp - hi.astype(F32)).astype(BF16)
        return jnp.concatenate([hi, lo], axis=1)


    def window_start(i):
        return (i + 1) * Q_ROWS - FAST_WINDOW

    def window_rows(i):
        return pl.ds(pl.multiple_of(jnp.maximum(window_start(i), 0), Q_ROWS), FAST_WINDOW)

    def store_block(i, acc):
        o_ref[0, 0, query_rows(i), :] = jnp.where(first_head, acc[:Q_ROWS], acc[Q_ROWS:]).astype(BF16)

    def stage_scores(g, buf):
        for slot in slots:
            i = g * GROUP_BLOCKS + slot
            z_ref[buf, slot] = lax.dot_general(stacked_queries(i), k_ref[0, 0, window_rows(i), :],
                                               nt_dims, preferred_element_type=F32)

    def stage_weights(buf, first_group):
        def visible(slot, b):
            i = slot if first_group else GROUP_BLOCKS
            bound = i * Q_ROWS - max(window_start(i), 0) - b * KEY_BLOCK
            return None if bound >= KEY_BLOCK else skew < bound

        zs, parts = [], []
        for slot in slots:
            z = z_ref[buf, slot]
            for b in range(WINDOW_BLOCKS):
                zb = hide(z[:, b * KEY_BLOCK:(b + 1) * KEY_BLOCK], visible(slot, b))
                zs.append(zb)
                parts.append(hi_lo(softplus(zb)))
        st = jnp.dot(jnp.concatenate(parts, axis=0), tri, preferred_element_type=F32)
        for slot in slots:
            carry = None
            a = [None] * WINDOW_BLOCKS
            for b in reversed(range(WINDOW_BLOCKS)):
                n = slot * WINDOW_BLOCKS + b
                sb = st[n * rows2:(n + 1) * rows2]
                x = zs[n] - sb[:, :KEY_BLOCK]
                if carry is not None:
                    x = x - carry
                a[b] = jnp.exp(x).astype(BF16)
                carry = sb[:, KEY_BLOCK:] if carry is None else carry + sb[:, KEY_BLOCK:]
            w_ref[buf, slot] = jnp.concatenate(a, axis=1)
            tot_ref[buf, slot] = carry

    def stage_values(g, buf, least):
        for slot in slots:
            i = g * GROUP_BLOCKS + slot
            acc = jnp.dot(w_ref[buf, slot], v_ref[0, 0, window_rows(i), :], preferred_element_type=F32)
            total = tot_ref[buf, slot]
            acc_ref[i] = acc
            carry_ref[i] = total
            store_block(i, acc)
            keys_left = window_start(i) > 0
            least = jnp.minimum(least, jnp.where(keys_left, jnp.min(total), jnp.inf))
        return least

    def step(g, buf, least):
        least = stage_values(g - 1, 1 - buf, least)
        stage_scores(g + 1, 1 - buf)
        stage_weights(buf, False)
        return least

    def step_pair(p, least):
        return step(2 + 2 * p, 0, step(1 + 2 * p, 1, least))

    assert n_groups >= 2 and n_groups * GROUP_BLOCKS * Q_ROWS == seq
    assert window_start(GROUP_BLOCKS) >= 0
    stage_scores(0, 0)
    stage_scores(1, 1)
    stage_weights(0, True)
    last = n_groups - 1
    least = lax.fori_loop(0, (last - 1) // 2, step_pair, jnp.float32(jnp.inf))
    if (last - 1) % 2:
        least = step(last - 1, (last - 1) % 2, least)
    least = stage_values(last - 1, (last - 1) % 2, least)
    stage_weights(last % 2, False)
    least = stage_values(last, last % 2, least)


    def sweep(i, start_raw, limit):
        start = pl.multiple_of(jnp.maximum(start_raw, 0), Q_ROWS)
        rows = pl.ds(start, KEY_BLOCK)
        z = lax.dot_general(stacked_queries(i), k_ref[0, 0, rows, :], nt_dims, preferred_element_type=F32)
        z = hide(z, jnp.logical_and(skew < i * Q_ROWS - start, col < limit - start))
        st = jnp.dot(hi_lo(softplus(z)), tri, preferred_element_type=F32)
        carry = carry_ref[i]
        a = jnp.exp(z - st[:, :KEY_BLOCK] - carry)
        acc_ref[i] += jnp.dot(a.astype(BF16), v_ref[0, 0, rows, :], preferred_element_type=F32)
        carry = carry + st[:, KEY_BLOCK:]
        carry_ref[i] = carry
        return jnp.min(carry)

    def finish_block(i, _):
        def more(state):
            start_raw, _, least = state
            return jnp.logical_and(start_raw > -KEY_BLOCK, least <= EXP_UNDERFLOW)

        def back(state):
            start_raw, limit, _ = state
            return start_raw - KEY_BLOCK, jnp.maximum(start_raw, 0), sweep(i, start_raw, limit)

        lax.while_loop(more, back, (window_start(i) - KEY_BLOCK, window_start(i), jnp.min(carry_ref[i])))
        store_block(i, acc_ref[i])
        return 0

    @pl.when(least <= EXP_UNDERFLOW)
    def _():
        lax.fori_loop(0, n_groups * GROUP_BLOCKS, finish_block, 0)


def _attention(qkv):
    B, _, S, _ = qkv.shape
    rows2 = HEADS_PER_SLAB * Q_ROWS
    def slab(which):
        return pl.BlockSpec((1, 1, S, LANES), lambda bi, p: (bi, which * N_SLABS + p, 0, 0))
    return pl.pallas_call(
        _attn_kernel,
        grid=(B, N_SLABS),
        in_specs=[slab(0), slab(1), slab(2)],
        out_specs=slab(0),
        out_shape=jax.ShapeDtypeStruct((B, N_SLABS, S, LANES), BF16),
        scratch_shapes=[pltpu.VMEM((S // Q_ROWS, rows2, LANES), F32),
                        pltpu.VMEM((S // Q_ROWS, rows2, KEY_BLOCK), F32),
                        pltpu.VMEM((2, GROUP_BLOCKS, rows2, FAST_WINDOW), F32),
                        pltpu.VMEM((2, GROUP_BLOCKS, rows2, FAST_WINDOW), BF16),
                        pltpu.VMEM((2, GROUP_BLOCKS, rows2, KEY_BLOCK), F32)],
        compiler_params=_params("parallel", "parallel"),
        name="stickbreak_attn",
    )(qkv, qkv, qkv)


def _layer1_kernel(o_ref, x_ref, wo_ref, g1_ref, b1_ref, wg_ref, wu_ref, wd_ref, g2_ref, b2_ref, y_ref):
    parts = range(LAYER1_PARTS)
    os_ = [jnp.concatenate([o_ref[0, k, _part_rows(p), :] for k in range(N_SLABS)], axis=1) for p in parts]
    ms = [jnp.dot(o, wo_ref[...], preferred_element_type=F32) for o in os_]
    hs = [_layer_norm(DEEPNORM_ALPHA * x_ref[0, _part_rows(p), :] + ms[p], _row(g1_ref, 1), _row(b1_ref, 1))
          for p in parts]
    ys = _ffn_sublayers(hs, wg_ref, wu_ref, wd_ref, _row(g2_ref, 1), _row(b2_ref, 1))
    for p in parts:
        y_ref[0, _part_rows(p), :] = ys[p]


def _layer1(o, x, w_o, g1, b1, wg, wu, wd, g2, b2):
    B, S, D = x.shape
    tile = LAYER1_PARTS * PART_ROWS
    row = pl.BlockSpec((1, tile, D), lambda bi, i: (bi, i, 0))
    consts = (w_o, g1, b1, wg, wu, wd, g2, b2)
    stacked = (wg, wu, wd)
    return pl.pallas_call(
        _layer1_kernel,
        grid=(B, S // tile),
        in_specs=[pl.BlockSpec((1, N_SLABS, tile, LANES), lambda bi, i: (bi, 0, i, 0)), row]
                 + [_resident_layer(c.shape, 1) if any(c is w for w in stacked) else _resident(c.shape)
                    for c in consts],
        out_specs=row,
        out_shape=jax.ShapeDtypeStruct(x.shape, F32),
        compiler_params=_params("parallel", "parallel"),
        name="proj_ffn",
    )(o, x, *consts)


def kernel(x, ln_mix_g, ln_mix_b, ln_ffn_g, ln_ffn_b, pool_w, pool_scale, w_qkv, w_o, w_gate, w_up, w_down):
    B, S, D = x.shape
    assert D == D_MODEL and S % (LAYER0_PARTS * PART_ROWS) == 0 and S % (LAYER1_PARTS * PART_ROWS) == 0
    assert S % (2 * GROUP_BLOCKS * Q_ROWS) == 0
    assert w_gate.shape[0] == DEPTH and pool_w.shape[0] == 1 and w_qkv.shape[0] == 1 and w_o.shape[0] == 1
    wg, wu, wd = w_gate.astype(BF16), w_up.astype(BF16), w_down.astype(BF16)
    h, qkv = _layer0(x, pool_w[0].astype(BF16), pool_scale, ln_mix_g, ln_mix_b, wg, wu, wd,
                     ln_ffn_g, ln_ffn_b, w_qkv[0].astype(BF16))
    o = _attention(qkv)
    return _layer1(o, h, w_o[0].astype(BF16), ln_mix_g, ln_mix_b, wg, wu, wd, ln_ffn_g, ln_ffn_b)
```
